```python
import jax, jax.numpy as jnp
from jax import lax
import numpy as np

D_MODEL = 1024
BATCH = 2
SEQ = 8192
DEPTH = 4

N_MIXERS = 2
N_ATTN_LAYERS = (DEPTH + 1) // 2
N_CONV_LAYERS = DEPTH // 2
N_HEADS = 8
QK_NOPE_DIM = 128
QK_ROPE_DIM = 64
QK_HEAD_DIM = QK_NOPE_DIM + QK_ROPE_DIM
V_HEAD_DIM = 128
Q_LORA_RANK = 256
KV_LORA_RANK = 128
DOWN_DIM = Q_LORA_RANK + KV_LORA_RANK + QK_ROPE_DIM
ROPE_THETA = 10000.0
Q_BLOCK = 128
CONV_WIDTH = 3
D_FF = 4 * D_MODEL
EPS = 1e-6
MAX_POS_OFFSET = 1024

kernel_name = 'hybrid_mla_shortconv_sqrelu'


def rms_norm(x, g):
    xf = x.astype(jnp.float32)
    xf = xf * lax.rsqrt(jnp.mean(xf * xf, axis=-1, keepdims=True) + EPS)
    return xf.astype(x.dtype) * g


def rope_tables(positions):
    inv_freq = ROPE_THETA ** (-jnp.arange(0, QK_ROPE_DIM, 2, dtype=jnp.float32) / QK_ROPE_DIM)
    ang = positions.astype(jnp.float32)[..., None] * inv_freq
    return jnp.cos(ang)[:, :, None, :], jnp.sin(ang)[:, :, None, :]


def apply_rope(x, cos, sin):
    x1, x2 = jnp.split(x, 2, axis=-1)
    cos = cos.astype(x.dtype)
    sin = sin.astype(x.dtype)
    return jnp.concatenate([x1 * cos - x2 * sin, x2 * cos + x1 * sin], axis=-1)


def mla_mixer(h, positions, cos, sin, w_down, g_q_a, g_kv_a, w_uq, w_ukv, g_qnorm, g_knorm, w_o):
    b, s, _ = h.shape
    a = h @ w_down
    c_q = rms_norm(a[..., :Q_LORA_RANK], g_q_a)
    c_kv = rms_norm(a[..., Q_LORA_RANK:Q_LORA_RANK + KV_LORA_RANK], g_kv_a)
    k_pe = a[..., Q_LORA_RANK + KV_LORA_RANK:]
    q = (c_q @ w_uq).reshape(b, s, N_HEADS, QK_HEAD_DIM)
    kv = (c_kv @ w_ukv).reshape(b, s, N_HEADS, QK_NOPE_DIM + V_HEAD_DIM)
    k_nope, v = kv[..., :QK_NOPE_DIM], kv[..., QK_NOPE_DIM:]
    k_pe = jnp.broadcast_to(k_pe[:, :, None, :], (b, s, N_HEADS, QK_ROPE_DIM))
    k = jnp.concatenate([k_nope, k_pe], axis=-1)
    q = rms_norm(q, g_qnorm)
    k = rms_norm(k, g_knorm)
    q = jnp.concatenate([q[..., :QK_NOPE_DIM], apply_rope(q[..., QK_NOPE_DIM:], cos, sin)], axis=-1)
    k = jnp.concatenate([k[..., :QK_NOPE_DIM], apply_rope(k[..., QK_NOPE_DIM:], cos, sin)], axis=-1)
    scale = QK_HEAD_DIM ** -0.5
    nb = s // Q_BLOCK
    q_blocks = q.reshape(b, nb, Q_BLOCK, N_HEADS, QK_HEAD_DIM).transpose(1, 0, 2, 3, 4)
    pos_blocks = positions.reshape(b, nb, Q_BLOCK).transpose(1, 0, 2)

    def attend(args):
        q_blk, pos_blk = args
        sc = jnp.einsum('bqhd,bkhd->bhqk', q_blk, k).astype(jnp.float32) * scale
        mask = pos_blk[:, None, :, None] >= positions[:, None, None, :]
        sc = jnp.where(mask, sc, jnp.finfo(jnp.float32).min)
        p = jax.nn.softmax(sc, axis=-1).astype(v.dtype)
        return jnp.einsum('bhqk,bkhd->bqhd', p, v)

    o = lax.map(attend, (q_blocks, pos_blocks))
    o = o.transpose(1, 0, 2, 3, 4).reshape(b, s, N_HEADS * V_HEAD_DIM)
    return o @ w_o


def short_conv_mixer(h, w_in, conv_w, w_out):
    bcu = h @ w_in
    gate_b, gate_c, u = jnp.split(bcu, 3, axis=-1)
    u = gate_c * u
    u = lax.conv_general_dilated(
        u, conv_w[:, None, :], window_strides=(1,), padding=((CONV_WIDTH - 1, 0),),
        dimension_numbers=('NWC', 'WIO', 'NWC'), feature_group_count=D_MODEL)
    return (gate_b * u) @ w_out


def sqrelu_mlp(h, w1, w2):
    return jnp.square(jax.nn.relu(h @ w1)) @ w2


def setup_inputs(seed: int = 0) -> dict:
    key = jax.random.key(seed)
    ks = jax.random.split(key, 20)
    nrm = lambda k, shape, fan_in: jax.random.normal(k, shape, jnp.float32) * fan_in ** -0.5
    gain = lambda k, shape: 1.0 + 0.02 * jax.random.normal(k, shape, jnp.float32)
    x = jax.random.normal(ks[0], (BATCH, SEQ, D_MODEL), jnp.float32)
    offset = jax.random.randint(ks[1], (BATCH, 1), 0, MAX_POS_OFFSET, dtype=jnp.int32)
    positions = offset + jnp.arange(SEQ, dtype=jnp.int32)[None, :]
    return {
        'x': x,
        'positions': positions,
        'g_mix': gain(ks[2], (DEPTH, D_MODEL)),
        'g_mlp': gain(ks[3], (DEPTH, D_MODEL)),
        'attn_w_down': nrm(ks[4], (N_ATTN_LAYERS, D_MODEL, DOWN_DIM), D_MODEL),
        'attn_g_q_a': gain(ks[5], (N_ATTN_LAYERS, Q_LORA_RANK)),
        'attn_g_kv_a': gain(ks[6], (N_ATTN_LAYERS, KV_LORA_RANK)),
        'attn_w_uq': nrm(ks[7], (N_ATTN_LAYERS, Q_LORA_RANK, N_HEADS * QK_HEAD_DIM), Q_LORA_RANK),
        'attn_w_ukv': nrm(ks[8], (N_ATTN_LAYERS, KV_LORA_RANK, N_HEADS * (QK_NOPE_DIM + V_HEAD_DIM)), KV_LORA_RANK),
        'attn_g_qnorm': gain(ks[9], (N_ATTN_LAYERS, QK_HEAD_DIM)),
        'attn_g_knorm': gain(ks[10], (N_ATTN_LAYERS, QK_HEAD_DIM)),
        'attn_w_o': nrm(ks[11], (N_ATTN_LAYERS, N_HEADS * V_HEAD_DIM, D_MODEL), N_HEADS * V_HEAD_DIM),
        'conv_w_in': nrm(ks[12], (N_CONV_LAYERS, D_MODEL, 3 * D_MODEL), D_MODEL),
        'conv_w': nrm(ks[13], (N_CONV_LAYERS, CONV_WIDTH, D_MODEL), CONV_WIDTH),
        'conv_w_out': nrm(ks[14], (N_CONV_LAYERS, D_MODEL, D_MODEL), D_MODEL),
        'mlp_w1': nrm(ks[15], (DEPTH, D_MODEL, D_FF), D_MODEL),
        'mlp_w2': nrm(ks[16], (DEPTH, D_FF, D_MODEL), D_FF),
    }


def reference(x, positions, g_mix, g_mlp, attn_w_down, attn_g_q_a, attn_g_kv_a, attn_w_uq,
              attn_w_ukv, attn_g_qnorm, attn_g_knorm, attn_w_o, conv_w_in, conv_w, conv_w_out,
              mlp_w1, mlp_w2):
    cos, sin = rope_tables(positions)
    for i in range(DEPTH):
        h = rms_norm(x, g_mix[i])
        j = i // N_MIXERS
        if i % N_MIXERS == 0:
            x = x + mla_mixer(h, positions, cos, sin, attn_w_down[j], attn_g_q_a[j], attn_g_kv_a[j],
                              attn_w_uq[j], attn_w_ukv[j], attn_g_qnorm[j], attn_g_knorm[j], attn_w_o[j])
        else:
            x = x + short_conv_mixer(h, conv_w_in[j], conv_w[j], conv_w_out[j])
        h = rms_norm(x, g_mlp[i])
        x = x + sqrelu_mlp(h, mlp_w1[i], mlp_w2[i])
    return x
```

```python
import functools
import math

import jax
import jax.numpy as jnp
from jax import lax
from jax.experimental import pallas as pl
from jax.experimental.pallas import tpu as pltpu

D_MODEL = 1024
N_HEADS = 8
NOPE = 128
ROPE = 64
QK_HEAD_DIM = NOPE + ROPE
V_DIM = 128
Q_LORA = 256
KV_LORA = 128
D_FF = 4096
CONV_WIDTH = 3
EPS = 1e-6
ROPE_THETA = 10000.0

LANES = 128
QK_PAD = 2 * LANES
HALO = 8
NEG = -1e30

TOKEN_TILE = 512
FF_CHUNK = 1024
Q_TILE = 512
KV_TILE = 512
VMEM_LIMIT = 56 * 1024 * 1024

F32 = jnp.float32
BF16 = jnp.bfloat16


def _rms(x):
    return x * lax.rsqrt(jnp.mean(x * x, axis=-1, keepdims=True) + EPS)


def _dot(a, b):
    return jnp.dot(a, b, preferred_element_type=F32)


def _resident(shape):
    return pl.BlockSpec(shape, lambda *_: (0,) * len(shape), pipeline_mode=pl.Buffered(1))


def _mlp_tail(x1, g_ref, w1_ref, w2_ref):
    h = (_rms(x1) * g_ref[...]).astype(BF16)
    acc = x1
    for c in range(D_FF // FF_CHUNK):
        cols = slice(c * FF_CHUNK, (c + 1) * FF_CHUNK)
        u = jnp.maximum(_dot(h, w1_ref[:, cols]), 0.0)
        acc = acc + _dot((u * u).astype(BF16), w2_ref[cols, :])
    return acc


def _qkv_kernel(x_ref, cos_ref, sin_ref, g_mix_ref, w_down_ref, g_qa_ref, g_kva_ref, w_q_ref, w_kv_ref,
                gq_n_ref, gq_r_ref, gq_rs_ref, gk_n_ref, gk_r_ref, gk_rs_ref, q_ref, k_ref, v_ref, *, q_scale):
    x = x_ref[0]
    h = (_rms(x) * g_mix_ref[...]).astype(BF16)
    a = _dot(h, w_down_ref[...])
    c_q = (_rms(a[:, :Q_LORA]) * g_qa_ref[...]).astype(BF16)
    c_kv = (_rms(a[:, Q_LORA:Q_LORA + KV_LORA]) * g_kva_ref[...]).astype(BF16)
    kpe = a[:, Q_LORA + KV_LORA:Q_LORA + KV_LORA + LANES]
    kpe_sw = a[:, Q_LORA + KV_LORA + LANES:]
    qa = _dot(c_q, w_q_ref[...])
    kva = _dot(c_kv, w_kv_ref[...])

    cos2 = cos_ref[0]
    sin2 = sin_ref[0]
    lane = lax.broadcasted_iota(jnp.int32, (1, LANES), 1)
    first = lane < ROPE
    inv_d = 1.0 / QK_HEAD_DIM

    k_rot = kpe * (gk_r_ref[...] * cos2) + kpe_sw * (gk_rs_ref[...] * sin2)
    ssq_kpe = jnp.sum(jnp.where(first, kpe * kpe, 0.0), axis=-1, keepdims=True)

    q_off = N_HEADS * NOPE
    for p in range(N_HEADS // 2):
        blk = slice(q_off + p * LANES, q_off + (p + 1) * LANES)
        blk_sw = slice(q_off + N_HEADS * ROPE + p * LANES, q_off + N_HEADS * ROPE + (p + 1) * LANES)
        qr = qa[:, blk]
        qr_sw = qa[:, blk_sw]
        q_rot = qr * (gq_r_ref[...] * cos2) + qr_sw * (gq_rs_ref[...] * sin2)
        qr2 = qr * qr
        ssq_even = jnp.sum(jnp.where(first, qr2, 0.0), axis=-1, keepdims=True)
        ssq_odd = jnp.sum(jnp.where(first, 0.0, qr2), axis=-1, keepdims=True)
        r_pair = []
        for e, ssq_r in ((0, ssq_even), (1, ssq_odd)):
            hd = 2 * p + e
            qn = qa[:, hd * NOPE:(hd + 1) * NOPE]
            r = lax.rsqrt((jnp.sum(qn * qn, axis=-1, keepdims=True) + ssq_r) * inv_d + EPS) * q_scale
            r_pair.append(r)
            q_ref[0, hd, :, :NOPE] = (qn * r * gq_n_ref[...]).astype(BF16)
        q_blk = (q_rot * jnp.where(first, r_pair[0], r_pair[1])).astype(BF16)
        q_ref[0, 2 * p, :, NOPE:] = q_blk
        q_ref[0, 2 * p + 1, :, NOPE:] = q_blk

    for hd in range(N_HEADS):
        kn = kva[:, hd * NOPE:(hd + 1) * NOPE]
        r = lax.rsqrt((jnp.sum(kn * kn, axis=-1, keepdims=True) + ssq_kpe) * inv_d + EPS)
        k_ref[0, hd, :, :NOPE] = (kn * r * gk_n_ref[...]).astype(BF16)
        keep = first if hd % 2 == 0 else jnp.logical_not(first)
        k_ref[0, hd, :, NOPE:] = jnp.where(keep, k_rot * r, 0.0).astype(BF16)
        v_ref[0, hd] = kva[:, N_HEADS * NOPE + hd * V_DIM:N_HEADS * NOPE + (hd + 1) * V_DIM].astype(BF16)


def _qkv_call(x, cos2, sin2, g_mix, w):
    b, s, _ = x.shape
    tm = TOKEN_TILE
    row = lambda n: _resident((1, n))
    q_scale = QK_HEAD_DIM ** -0.5 * math.log2(math.e)
    out_shape = (jax.ShapeDtypeStruct((b, N_HEADS, s, QK_PAD), BF16),
                 jax.ShapeDtypeStruct((b, N_HEADS, s, QK_PAD), BF16),
                 jax.ShapeDtypeStruct((b, N_HEADS, s, V_DIM), BF16))
    return pl.pallas_call(
        functools.partial(_qkv_kernel, q_scale=q_scale),
        grid=(b, s // tm),
        in_specs=[
            pl.BlockSpec((1, tm, D_MODEL), lambda bi, i: (bi, i, 0)),
            pl.BlockSpec((1, tm, LANES), lambda bi, i: (bi, i, 0)),
            pl.BlockSpec((1, tm, LANES), lambda bi, i: (bi, i, 0)),
            row(D_MODEL),
            _resident(w['w_down'].shape), row(Q_LORA), row(KV_LORA),
            _resident(w['w_q'].shape), _resident(w['w_kv'].shape),
            row(NOPE), row(LANES), row(LANES), row(NOPE), row(LANES), row(LANES),
        ],
        out_specs=(
            pl.BlockSpec((1, N_HEADS, tm, QK_PAD), lambda bi, i: (bi, 0, i, 0)),
            pl.BlockSpec((1, N_HEADS, tm, QK_PAD), lambda bi, i: (bi, 0, i, 0)),
            pl.BlockSpec((1, N_HEADS, tm, V_DIM), lambda bi, i: (bi, 0, i, 0)),
        ),
        out_shape=out_shape,
        compiler_params=pltpu.CompilerParams(
            dimension_semantics=("parallel", "parallel"), vmem_limit_bytes=VMEM_LIMIT),
        name="qkv_proj",
    )(x, cos2, sin2, g_mix, w['w_down'], w['g_qa'], w['g_kva'], w['w_q'], w['w_kv'],
      w['gq_n'], w['gq_r'], w['gq_rs'], w['gk_n'], w['gk_r'], w['gk_rs'])


def _flash_kernel(q_ref, k_ref, v_ref, o_ref, m_ref, l_ref, acc_ref):
    i = pl.program_id(2)
    tq, tk = Q_TILE, KV_TILE
    q = q_ref[0, 0]
    m_ref[...] = jnp.full(m_ref.shape, NEG, F32)
    l_ref[...] = jnp.zeros(l_ref.shape, F32)
    acc_ref[...] = jnp.zeros(acc_ref.shape, F32)

    def step(off, mask):
        k = k_ref[0, 0, pl.ds(off, tk), :]
        v = v_ref[0, 0, pl.ds(off, tk), :]
        s = lax.dot_general(q, k, (((1,), (1,)), ((), ())), preferred_element_type=F32)
        if mask is not None:
            s = jnp.where(mask, s, NEG)
        m_old = m_ref[...]
        m_new = jnp.maximum(m_old, jnp.max(s, axis=-1, keepdims=True))
        alpha = jnp.exp2(m_old - m_new)
        p = jnp.exp2(s - m_new)
        l_ref[...] = alpha * l_ref[...] + jnp.sum(p, axis=-1, keepdims=True)
        acc_ref[...] = alpha * acc_ref[...] + _dot(p.astype(BF16), v)
        m_ref[...] = m_new

    def full_block(j, carry):
        step(pl.multiple_of(j * tk, tk), None)
        return carry

    lax.fori_loop(0, i * (tq // tk), full_block, 0)
    row = lax.broadcasted_iota(jnp.int32, (tq, tk), 0)
    col = lax.broadcasted_iota(jnp.int32, (tq, tk), 1)
    for d in range(tq // tk):
        step(pl.multiple_of(i * tq + d * tk, tk), row >= col + d * tk)
    o_ref[0] = (acc_ref[...] / l_ref[...]).astype(BF16)


def _flash_call(q, k, v):
    b, nh, s, _ = q.shape
    tq = Q_TILE
    return pl.pallas_call(
        _flash_kernel,
        grid=(b, nh, s // tq),
        in_specs=[
            pl.BlockSpec((1, 1, tq, QK_PAD), lambda bi, h, i: (bi, h, i, 0)),
            pl.BlockSpec((1, 1, s, QK_PAD), lambda bi, h, i: (bi, h, 0, 0)),
            pl.BlockSpec((1, 1, s, V_DIM), lambda bi, h, i: (bi, h, 0, 0)),
        ],
        out_specs=pl.BlockSpec((1, tq, V_DIM), lambda bi, h, i: (bi, i, h)),
        out_shape=jax.ShapeDtypeStruct((b, s, nh * V_DIM), BF16),
        scratch_shapes=[pltpu.VMEM((tq, 1), F32), pltpu.VMEM((tq, 1), F32), pltpu.VMEM((tq, V_DIM), F32)],
        compiler_params=pltpu.CompilerParams(
            dimension_semantics=("parallel", "parallel", "arbitrary"), vmem_limit_bytes=VMEM_LIMIT),
        name="flash_attn",
    )(q, k, v)


def _attn_out_mlp_kernel(x_ref, o_ref, w_o_ref, g_ref, w1_ref, w2_ref, out_ref):
    x1 = x_ref[...] + _dot(o_ref[...], w_o_ref[...])
    out_ref[...] = _mlp_tail(x1, g_ref, w1_ref, w2_ref)


def _attn_out_mlp_call(x2d, o2d, w_o, g_mlp, w1, w2):
    t, _ = x2d.shape
    tm = TOKEN_TILE
    return pl.pallas_call(
        _attn_out_mlp_kernel,
        grid=(t // tm,),
        in_specs=[
            pl.BlockSpec((tm, D_MODEL), lambda i: (i, 0)),
            pl.BlockSpec((tm, N_HEADS * V_DIM), lambda i: (i, 0)),
            _resident(w_o.shape), _resident((1, D_MODEL)), _resident(w1.shape), _resident(w2.shape),
        ],
        out_specs=pl.BlockSpec((tm, D_MODEL), lambda i: (i, 0)),
        out_shape=jax.ShapeDtypeStruct((t, D_MODEL), F32),
        compiler_params=pltpu.CompilerParams(dimension_semantics=("parallel",), vmem_limit_bytes=VMEM_LIMIT),
        name="attn_out_mlp",
    )(x2d, o2d, w_o, g_mlp, w1, w2)


def _conv_mlp_kernel(x_ref, g_mix_ref, w_in_ref, cw_ref, w_out_ref, g_ref, w1_ref, w2_ref, out_ref, u_buf):
    tm = x_ref.shape[1]

    @pl.when(pl.program_id(1) == 0)
    def _():
        u_buf[:HALO, :] = jnp.zeros((HALO, D_MODEL), F32)

    x = x_ref[0]
    h = (_rms(x) * g_mix_ref[...]).astype(BF16)
    gate_c = _dot(h, w_in_ref[:, D_MODEL:2 * D_MODEL])
    u = gate_c * _dot(h, w_in_ref[:, 2 * D_MODEL:])
    u_buf[HALO:, :] = u
    conv = (u_buf[HALO - 2:HALO - 2 + tm, :] * cw_ref[0:1, :]
            + u_buf[HALO - 1:HALO - 1 + tm, :] * cw_ref[1:2, :]
            + u * cw_ref[2:3, :])
    u_buf[:HALO, :] = u_buf[tm:tm + HALO, :]
    gate_b = _dot(h, w_in_ref[:, :D_MODEL])
    x1 = x + _dot((gate_b * conv).astype(BF16), w_out_ref[...])
    out_ref[0] = _mlp_tail(x1, g_ref, w1_ref, w2_ref)


def _conv_mlp_call(x, g_mix, w_in, conv_w, w_out, g_mlp, w1, w2):
    b, s, _ = x.shape
    tm = TOKEN_TILE
    return pl.pallas_call(
        _conv_mlp_kernel,
        grid=(b, s // tm),
        in_specs=[
            pl.BlockSpec((1, tm, D_MODEL), lambda bi, i: (bi, i, 0)),
            _resident((1, D_MODEL)), _resident(w_in.shape), _resident(conv_w.shape), _resident(w_out.shape),
            _resident((1, D_MODEL)), _resident(w1.shape), _resident(w2.shape),
        ],
        out_specs=pl.BlockSpec((1, tm, D_MODEL), lambda bi, i: (bi, i, 0)),
        out_shape=jax.ShapeDtypeStruct((b, s, D_MODEL), F32),
        scratch_shapes=[pltpu.VMEM((tm + HALO, D_MODEL), F32)],
        compiler_params=pltpu.CompilerParams(
            dimension_semantics=("parallel", "arbitrary"), vmem_limit_bytes=VMEM_LIMIT),
        name="conv_mlp",
    )(x, g_mix, w_in, conv_w, w_out, g_mlp, w1, w2)


def _swap_halves(w):
    half = w.shape[-1] // 2
    return jnp.concatenate([w[..., half:], w[..., :half]], axis=-1)


def _attn_params(w_down, g_q_a, g_kv_a, w_uq, w_ukv, g_qnorm, g_knorm):
    kpe = w_down[:, Q_LORA + KV_LORA:]
    kpe_sw = _swap_halves(kpe)
    w_down_ext = jnp.concatenate([w_down[:, :Q_LORA + KV_LORA], kpe, kpe, kpe_sw, kpe_sw], axis=1)
    uq = w_uq.reshape(Q_LORA, N_HEADS, QK_HEAD_DIM)
    uq_r = uq[:, :, NOPE:]
    w_q = jnp.concatenate([uq[:, :, :NOPE].reshape(Q_LORA, -1), uq_r.reshape(Q_LORA, -1),
                           _swap_halves(uq_r).reshape(Q_LORA, -1)], axis=1)
    ukv = w_ukv.reshape(KV_LORA, N_HEADS, NOPE + V_DIM)
    w_kv = jnp.concatenate([ukv[:, :, :NOPE].reshape(KV_LORA, -1), ukv[:, :, NOPE:].reshape(KV_LORA, -1)], axis=1)
    pair = lambda g: jnp.concatenate([g, g])[None, :]
    return {
        'w_down': w_down_ext.astype(BF16), 'g_qa': g_q_a[None, :], 'g_kva': g_kv_a[None, :],
        'w_q': w_q.astype(BF16), 'w_kv': w_kv.astype(BF16),
        'gq_n': g_qnorm[None, :NOPE], 'gq_r': pair(g_qnorm[NOPE:]), 'gq_rs': pair(_swap_halves(g_qnorm[NOPE:])),
        'gk_n': g_knorm[None, :NOPE], 'gk_r': pair(g_knorm[NOPE:]), 'gk_rs': pair(_swap_halves(g_knorm[NOPE:])),
    }


def _rope_tables(positions):
    inv_freq = ROPE_THETA ** (-jnp.arange(0, ROPE, 2, dtype=F32) / ROPE)
    ang = positions.astype(F32)[..., None] * inv_freq
    cos, sin = jnp.cos(ang), jnp.sin(ang)
    return jnp.concatenate([cos] * 4, axis=-1), jnp.concatenate([-sin, sin] * 2, axis=-1)


def kernel(x, positions, g_mix, g_mlp, attn_w_down, attn_g_q_a, attn_g_kv_a, attn_w_uq, attn_w_ukv, attn_g_qnorm,
           attn_g_knorm, attn_w_o, conv_w_in, conv_w, conv_w_out, mlp_w1, mlp_w2):
    b, s, d = x.shape
    depth = g_mix.shape[0]
    cos2, sin2 = _rope_tables(positions)
    for i in range(depth):
        j = i // 2
        w1 = mlp_w1[i].astype(BF16)
        w2 = mlp_w2[i].astype(BF16)
        if i % 2 == 0:
            w = _attn_params(attn_w_down[j], attn_g_q_a[j], attn_g_kv_a[j], attn_w_uq[j], attn_w_ukv[j],
                             attn_g_qnorm[j], attn_g_knorm[j])
            q, k, v = _qkv_call(x, cos2, sin2, g_mix[i][None, :], w)
            o = _flash_call(q, k, v)
            x = _attn_out_mlp_call(x.reshape(b * s, d), o.reshape(b * s, d), attn_w_o[j].astype(BF16),
                                   g_mlp[i][None, :], w1, w2).reshape(b, s, d)
        else:
            x = _conv_mlp_call(x, g_mix[i][None, :], conv_w_in[j].astype(BF16), conv_w[j],
                               conv_w_out[j].astype(BF16), g_mlp[i][None, :], w1, w2)
    return x
```

```python
import functools
import math

import jax
import jax.numpy as jnp
from jax import lax
from jax.experimental import pallas as pl
from jax.experimental.pallas import tpu as pltpu

D_MODEL = 1024
N_HEADS = 8
NOPE = 128
ROPE = 64
QK_HEAD_DIM = NOPE + ROPE
V_DIM = 128
Q_LORA = 256
KV_LORA = 128
D_FF = 4096
CONV_WIDTH = 3
EPS = 1e-6
ROPE_THETA = 10000.0

LANES = 128
QK_PAD = 2 * LANES
HALO = 8
NEG = -1e30

Q_SCALE = QK_HEAD_DIM ** -0.5 * math.log2(math.e)
MAX_FIXED_REF_SPAN = 100.0
BOUND_MARGIN = 1.02

TOKEN_TILE = 512
FF_CHUNK = 1024
Q_TILE = 1024
KV_STEP = 1024
DIAG_TILE = 256
FALLBACK_KV_TILE = 512
VMEM_LIMIT = 56 * 1024 * 1024

F32 = jnp.float32
BF16 = jnp.bfloat16


def _rms(x):
    return x * lax.rsqrt(jnp.mean(x * x, axis=-1, keepdims=True) + EPS)


def _dot(a, b):
    return jnp.dot(a, b, preferred_element_type=F32)


def _resident(shape):
    return pl.BlockSpec(shape, lambda *_: (0,) * len(shape), pipeline_mode=pl.Buffered(1))


def _mlp_tail(x1, g_ref, w1_ref, w2_ref):
    h = (_rms(x1) * g_ref[...]).astype(BF16)
    acc = x1
    for c in range(D_FF // FF_CHUNK):
        cols = slice(c * FF_CHUNK, (c + 1) * FF_CHUNK)
        u = jnp.maximum(_dot(h, w1_ref[:, cols]), 0.0)
        acc = acc + _dot((u * u).astype(BF16), w2_ref[cols, :])
    return acc


def _qkv_kernel(x_ref, cos_ref, sin_ref, g_mix_ref, w_down_ref, g_qa_ref, g_kva_ref, w_q_ref, w_kv_ref,
                gq_n_ref, gq_r_ref, gq_rs_ref, gk_n_ref, gk_r_ref, gk_rs_ref, q_pad_ref, q_ref, k_ref, v_ref,
                *, q_scale):
    x = x_ref[0]
    h = (_rms(x) * g_mix_ref[...]).astype(BF16)
    a = _dot(h, w_down_ref[...])
    c_q = (_rms(a[:, :Q_LORA]) * g_qa_ref[...]).astype(BF16)
    c_kv = (_rms(a[:, Q_LORA:Q_LORA + KV_LORA]) * g_kva_ref[...]).astype(BF16)
    kpe = a[:, Q_LORA + KV_LORA:Q_LORA + KV_LORA + LANES]
    kpe_sw = a[:, Q_LORA + KV_LORA + LANES:]
    qa = _dot(c_q, w_q_ref[...])
    kva = _dot(c_kv, w_kv_ref[...])

    cos2 = cos_ref[0]
    sin2 = sin_ref[0]
    lane = lax.broadcasted_iota(jnp.int32, (1, LANES), 1)
    is_rope = lane < ROPE
    k_pad = jnp.where(lane == ROPE, 1.0, 0.0)
    inv_d = 1.0 / QK_HEAD_DIM

    k_rot = kpe * (gk_r_ref[...] * cos2) + kpe_sw * (gk_rs_ref[...] * sin2)
    ssq_kpe = jnp.sum(kpe * kpe, axis=-1, keepdims=True)
    q_cos = gq_r_ref[...] * cos2
    q_sin = gq_rs_ref[...] * sin2

    q_off = N_HEADS * NOPE
    for hd in range(N_HEADS):
        qn = qa[:, hd * NOPE:(hd + 1) * NOPE]
        qr = qa[:, q_off + hd * LANES:q_off + (hd + 1) * LANES]
        qr_sw = qa[:, q_off + (N_HEADS + hd) * LANES:q_off + (N_HEADS + hd + 1) * LANES]
        ssq = jnp.sum(qn * qn, axis=-1, keepdims=True) + jnp.sum(qr * qr, axis=-1, keepdims=True)
        r = lax.rsqrt(ssq * inv_d + EPS) * q_scale
        q_ref[0, hd, :, :NOPE] = (qn * r * gq_n_ref[...]).astype(BF16)
        q_rot = (qr * q_cos + qr_sw * q_sin) * r
        q_ref[0, hd, :, NOPE:] = jnp.where(is_rope, q_rot, q_pad_ref[...]).astype(BF16)

        kn = kva[:, hd * NOPE:(hd + 1) * NOPE]
        r = lax.rsqrt((jnp.sum(kn * kn, axis=-1, keepdims=True) + ssq_kpe) * inv_d + EPS)
        k_ref[0, hd, :, :NOPE] = (kn * r * gk_n_ref[...]).astype(BF16)
        k_ref[0, hd, :, NOPE:] = jnp.where(is_rope, k_rot * r, k_pad).astype(BF16)
        v_ref[0, hd] = kva[:, N_HEADS * NOPE + hd * V_DIM:N_HEADS * NOPE + (hd + 1) * V_DIM].astype(BF16)


def _qkv_call(x, cos2, sin2, g_mix, w):
    b, s, _ = x.shape
    tm = TOKEN_TILE
    row = lambda n: _resident((1, n))
    out_shape = (jax.ShapeDtypeStruct((b, N_HEADS, s, QK_PAD), BF16),
                 jax.ShapeDtypeStruct((b, N_HEADS, s, QK_PAD), BF16),
                 jax.ShapeDtypeStruct((b, N_HEADS, s, V_DIM), BF16))
    return pl.pallas_call(
        functools.partial(_qkv_kernel, q_scale=Q_SCALE),
        grid=(b, s // tm),
        in_specs=[
            pl.BlockSpec((1, tm, D_MODEL), lambda bi, i: (bi, i, 0)),
            pl.BlockSpec((1, tm, LANES), lambda bi, i: (bi, i, 0)),
            pl.BlockSpec((1, tm, LANES), lambda bi, i: (bi, i, 0)),
            row(D_MODEL),
            _resident(w['w_down'].shape), row(Q_LORA), row(KV_LORA),
            _resident(w['w_q'].shape), _resident(w['w_kv'].shape),
            row(NOPE), row(LANES), row(LANES), row(NOPE), row(LANES), row(LANES), row(LANES),
        ],
        out_specs=(
            pl.BlockSpec((1, N_HEADS, tm, QK_PAD), lambda bi, i: (bi, 0, i, 0)),
            pl.BlockSpec((1, N_HEADS, tm, QK_PAD), lambda bi, i: (bi, 0, i, 0)),
            pl.BlockSpec((1, N_HEADS, tm, V_DIM), lambda bi, i: (bi, 0, i, 0)),
        ),
        out_shape=out_shape,
        compiler_params=pltpu.CompilerParams(
            dimension_semantics=("parallel", "parallel"), vmem_limit_bytes=VMEM_LIMIT),
        name="qkv_proj",
    )(x, cos2, sin2, g_mix, w['w_down'], w['g_qa'], w['g_kva'], w['w_q'], w['w_kv'],
      w['gq_n'], w['gq_r'], w['gq_rs'], w['gk_n'], w['gk_r'], w['gk_rs'], w['q_pad'])


def _dot_nt(a, b):
    return lax.dot_general(a, b, (((1,), (1,)), ((), ())), preferred_element_type=F32)


def _flash_fixed_ref(q_ref, k_ref, v_ref, o_ref, acc_ref):
    i = pl.program_id(2)
    tq, step, dg = Q_TILE, KV_STEP, DIAG_TILE
    lane = lax.broadcasted_iota(jnp.int32, (1, LANES), 1)
    ones_col = jnp.where(lane == 0, 1.0, 0.0).astype(BF16)

    def weighted_values(p, off, n):
        v = jnp.concatenate([v_ref[0, 0, pl.ds(off, n), :], jnp.broadcast_to(ones_col, (n, LANES))], axis=1)
        return _dot(p.astype(BF16), v)

    acc_ref[...] = jnp.zeros(acc_ref.shape, F32)

    def full_step(j, carry):
        off = pl.multiple_of(j * step, step)
        s = _dot_nt(q_ref[0, 0], k_ref[0, 0, pl.ds(off, step), :])
        acc_ref[...] += weighted_values(jnp.exp2(s), off, step)
        return carry

    lax.fori_loop(0, i * (tq // step), full_step, 0)

    base = i * tq
    causal = (lax.broadcasted_iota(jnp.int32, (dg, dg), 0) >= lax.broadcasted_iota(jnp.int32, (dg, dg), 1))
    for r in range(tq // dg):
        rows = slice(r * dg, (r + 1) * dg)
        q = q_ref[0, 0, rows, :]
        if r > 0:
            off = pl.multiple_of(base, tq)
            s = _dot_nt(q, k_ref[0, 0, pl.ds(off, r * dg), :])
            acc_ref[rows, :] += weighted_values(jnp.exp2(s), off, r * dg)
        off = pl.multiple_of(base + r * dg, dg)
        s = jnp.where(causal, _dot_nt(q, k_ref[0, 0, pl.ds(off, dg), :]), NEG)
        acc_ref[rows, :] += weighted_values(jnp.exp2(s), off, dg)

    acc = acc_ref[...]
    o_ref[0] = (acc[:, :V_DIM] / acc[:, V_DIM:V_DIM + 1]).astype(BF16)


def _flash_running_max(q_ref, k_ref, v_ref, o_ref, m_ref, l_ref, acc_ref):
    i = pl.program_id(2)
    tq, tk = Q_TILE, FALLBACK_KV_TILE
    m_ref[...] = jnp.full(m_ref.shape, NEG, F32)
    l_ref[...] = jnp.zeros(l_ref.shape, F32)
    acc_ref[:, :V_DIM] = jnp.zeros((tq, V_DIM), F32)

    def step(off, mask):
        s = _dot_nt(q_ref[0, 0], k_ref[0, 0, pl.ds(off, tk), :])
        if mask is not None:
            s = jnp.where(mask, s, NEG)
        m_old = m_ref[...]
        m_new = jnp.maximum(m_old, jnp.max(s, axis=-1, keepdims=True))
        alpha = jnp.exp2(m_old - m_new)
        p = jnp.exp2(s - m_new)
        l_ref[...] = alpha * l_ref[...] + jnp.sum(p, axis=-1, keepdims=True)
        acc_ref[:, :V_DIM] = alpha * acc_ref[:, :V_DIM] + _dot(p.astype(BF16), v_ref[0, 0, pl.ds(off, tk), :])
        m_ref[...] = m_new

    def full_step(j, carry):
        step(pl.multiple_of(j * tk, tk), None)
        return carry

    lax.fori_loop(0, i * (tq // tk), full_step, 0)
    row = lax.broadcasted_iota(jnp.int32, (tq, tk), 0)
    col = lax.broadcasted_iota(jnp.int32, (tq, tk), 1)
    for d in range(tq // tk):
        step(pl.multiple_of(i * tq + d * tk, tk), row >= col + d * tk)
    o_ref[0] = (acc_ref[:, :V_DIM] / l_ref[...]).astype(BF16)


def _flash_kernel(fixed_ref_ok, q_ref, k_ref, v_ref, o_ref, m_ref, l_ref, acc_ref):
    @pl.when(fixed_ref_ok[0] == 1)
    def _():
        _flash_fixed_ref(q_ref, k_ref, v_ref, o_ref, acc_ref)

    @pl.when(fixed_ref_ok[0] == 0)
    def _():
        _flash_running_max(q_ref, k_ref, v_ref, o_ref, m_ref, l_ref, acc_ref)


def _flash_call(fixed_ref_ok, q, k, v):
    b, nh, s, _ = q.shape
    tq = Q_TILE
    grid_spec = pltpu.PrefetchScalarGridSpec(
        num_scalar_prefetch=1,
        grid=(b, nh, s // tq),
        in_specs=[
            pl.BlockSpec((1, 1, tq, QK_PAD), lambda bi, h, i, _: (bi, h, i, 0)),
            pl.BlockSpec((1, 1, s, QK_PAD), lambda bi, h, i, _: (bi, h, 0, 0)),
            pl.BlockSpec((1, 1, s, V_DIM), lambda bi, h, i, _: (bi, h, 0, 0)),
        ],
        out_specs=pl.BlockSpec((1, tq, V_DIM), lambda bi, h, i, _: (bi, i, h)),
        scratch_shapes=[pltpu.VMEM((tq, 1), F32), pltpu.VMEM((tq, 1), F32), pltpu.VMEM((tq, 2 * LANES), F32)],
    )
    return pl.pallas_call(
        _flash_kernel,
        grid_spec=grid_spec,
        out_shape=jax.ShapeDtypeStruct((b, s, nh * V_DIM), BF16),
        compiler_params=pltpu.CompilerParams(
            dimension_semantics=("parallel", "parallel", "arbitrary"), vmem_limit_bytes=VMEM_LIMIT),
        name="flash_attn",
    )(fixed_ref_ok, q, k, v)


def _attn_out_mlp_kernel(x_ref, o_ref, w_o_ref, g_ref, w1_ref, w2_ref, out_ref):
    x1 = x_ref[...] + _dot(o_ref[...], w_o_ref[...])
    out_ref[...] = _mlp_tail(x1, g_ref, w1_ref, w2_ref)


def _attn_out_mlp_call(x2d, o2d, w_o, g_mlp, w1, w2):
    t, _ = x2d.shape
    tm = TOKEN_TILE
    return pl.pallas_call(
        _attn_out_mlp_kernel,
        grid=(t // tm,),
        in_specs=[
            pl.BlockSpec((tm, D_MODEL), lambda i: (i, 0)),
            pl.BlockSpec((tm, N_HEADS * V_DIM), lambda i: (i, 0)),
            _resident(w_o.shape), _resident((1, D_MODEL)), _resident(w1.shape), _resident(w2.shape),
        ],
        out_specs=pl.BlockSpec((tm, D_MODEL), lambda i: (i, 0)),
        out_shape=jax.ShapeDtypeStruct((t, D_MODEL), F32),
        compiler_params=pltpu.CompilerParams(dimension_semantics=("parallel",), vmem_limit_bytes=VMEM_LIMIT),
        name="attn_out_mlp",
    )(x2d, o2d, w_o, g_mlp, w1, w2)


def _conv_mlp_kernel(x_ref, g_mix_ref, w_in_ref, cw_ref, w_out_ref, g_ref, w1_ref, w2_ref, out_ref, u_buf):
    tm = x_ref.shape[1]

    @pl.when(pl.program_id(1) == 0)
    def _():
        u_buf[:HALO, :] = jnp.zeros((HALO, D_MODEL), F32)

    x = x_ref[0]
    h = (_rms(x) * g_mix_ref[...]).astype(BF16)
    gate_c = _dot(h, w_in_ref[:, D_MODEL:2 * D_MODEL])
    u = gate_c * _dot(h, w_in_ref[:, 2 * D_MODEL:])
    u_buf[HALO:, :] = u
    conv = (u_buf[HALO - 2:HALO - 2 + tm, :] * cw_ref[0:1, :]
            + u_buf[HALO - 1:HALO - 1 + tm, :] * cw_ref[1:2, :]
            + u * cw_ref[2:3, :])
    u_buf[:HALO, :] = u_buf[tm:tm + HALO, :]
    gate_b = _dot(h, w_in_ref[:, :D_MODEL])
    x1 = x + _dot((gate_b * conv).astype(BF16), w_out_ref[...])
    out_ref[0] = _mlp_tail(x1, g_ref, w1_ref, w2_ref)


def _conv_mlp_call(x, g_mix, w_in, conv_w, w_out, g_mlp, w1, w2):
    b, s, _ = x.shape
    tm = TOKEN_TILE
    return pl.pallas_call(
        _conv_mlp_kernel,
        grid=(b, s // tm),
        in_specs=[
            pl.BlockSpec((1, tm, D_MODEL), lambda bi, i: (bi, i, 0)),
            _resident((1, D_MODEL)), _resident(w_in.shape), _resident(conv_w.shape), _resident(w_out.shape),
            _resident((1, D_MODEL)), _resident(w1.shape), _resident(w2.shape),
        ],
        out_specs=pl.BlockSpec((1, tm, D_MODEL), lambda bi, i: (bi, i, 0)),
        out_shape=jax.ShapeDtypeStruct((b, s, D_MODEL), F32),
        scratch_shapes=[pltpu.VMEM((tm + HALO, D_MODEL), F32)],
        compiler_params=pltpu.CompilerParams(
            dimension_semantics=("parallel", "arbitrary"), vmem_limit_bytes=VMEM_LIMIT),
        name="conv_mlp",
    )(x, g_mix, w_in, conv_w, w_out, g_mlp, w1, w2)


def _swap_halves(w):
    half = w.shape[-1] // 2
    return jnp.concatenate([w[..., half:], w[..., :half]], axis=-1)


def _attn_params(w_down, g_q_a, g_kv_a, w_uq, w_ukv, g_qnorm, g_knorm):
    pad = lambda w: jnp.pad(w, [(0, 0)] * (w.ndim - 1) + [(0, LANES - w.shape[-1])])
    kpe = w_down[:, Q_LORA + KV_LORA:]
    w_down_ext = jnp.concatenate([w_down[:, :Q_LORA + KV_LORA], pad(kpe), pad(_swap_halves(kpe))], axis=1)
    uq = w_uq.reshape(Q_LORA, N_HEADS, QK_HEAD_DIM)
    uq_r = uq[:, :, NOPE:]
    w_q = jnp.concatenate([uq[:, :, :NOPE].reshape(Q_LORA, -1), pad(uq_r).reshape(Q_LORA, -1),
                           pad(_swap_halves(uq_r)).reshape(Q_LORA, -1)], axis=1)
    ukv = w_ukv.reshape(KV_LORA, N_HEADS, NOPE + V_DIM)
    w_kv = jnp.concatenate([ukv[:, :, :NOPE].reshape(KV_LORA, -1), ukv[:, :, NOPE:].reshape(KV_LORA, -1)], axis=1)
    row = lambda g: pad(g)[None, :]
    bound = (BOUND_MARGIN * Q_SCALE * QK_HEAD_DIM) * jnp.max(jnp.abs(g_qnorm)) * jnp.max(jnp.abs(g_knorm))
    fixed_ref_ok = (2.0 * bound < MAX_FIXED_REF_SPAN).astype(jnp.int32).reshape(1)
    q_pad = jnp.where(jnp.arange(LANES) == ROPE, -bound, 0.0)[None, :]
    return {
        'w_down': w_down_ext.astype(BF16), 'g_qa': g_q_a[None, :], 'g_kva': g_kv_a[None, :],
        'w_q': w_q.astype(BF16), 'w_kv': w_kv.astype(BF16),
        'gq_n': g_qnorm[None, :NOPE], 'gq_r': row(g_qnorm[NOPE:]), 'gq_rs': row(_swap_halves(g_qnorm[NOPE:])),
        'gk_n': g_knorm[None, :NOPE], 'gk_r': row(g_knorm[NOPE:]), 'gk_rs': row(_swap_halves(g_knorm[NOPE:])),
        'q_pad': q_pad, 'fixed_ref_ok': fixed_ref_ok,
    }


def _rope_tables(positions):
    inv_freq = ROPE_THETA ** (-jnp.arange(0, ROPE, 2, dtype=F32) / ROPE)
    ang = positions.astype(F32)[..., None] * inv_freq
    cos, sin = jnp.cos(ang), jnp.sin(ang)
    return jnp.concatenate([cos] * 4, axis=-1), jnp.concatenate([-sin, sin] * 2, axis=-1)


def kernel(x, positions, g_mix, g_mlp, attn_w_down, attn_g_q_a, attn_g_kv_a, attn_w_uq, attn_w_ukv, attn_g_qnorm,
           attn_g_knorm, attn_w_o, conv_w_in, conv_w, conv_w_out, mlp_w1, mlp_w2):
    b, s, d = x.shape
    depth = g_mix.shape[0]
    cos2, sin2 = _rope_tables(positions)
    for i in range(depth):
        j = i // 2
        w1 = mlp_w1[i].astype(BF16)
        w2 = mlp_w2[i].astype(BF16)
        if i % 2 == 0:
            w = _attn_params(attn_w_down[j], attn_g_q_a[j], attn_g_kv_a[j], attn_w_uq[j], attn_w_ukv[j],
                             attn_g_qnorm[j], attn_g_knorm[j])
            q, k, v = _qkv_call(x, cos2, sin2, g_mix[i][None, :], w)
            o = _flash_call(w['fixed_ref_ok'], q, k, v)
            x = _attn_out_mlp_call(x.reshape(b * s, d), o.reshape(b * s, d), attn_w_o[j].astype(BF16),
                                   g_mlp[i][None, :], w1, w2).reshape(b, s, d)
        else:
            x = _conv_mlp_call(x, g_mix[i][None, :], conv_w_in[j].astype(BF16), conv_w[j],
                               conv_w_out[j].astype(BF16), g_mlp[i][None, :], w1, w2)
    return x
```

```python
import functools
import math

import jax
import jax.numpy as jnp
from jax import lax
from jax.experimental import pallas as pl
from jax.experimental.pallas import tpu as pltpu

D_MODEL = 1024
N_HEADS = 8
NOPE = 128
ROPE = 64
QK_HEAD_DIM = NOPE + ROPE
V_DIM = 128
Q_LORA = 256
KV_LORA = 128
D_FF = 4096
CONV_WIDTH = 3
EPS = 1e-6
ROPE_THETA = 10000.0

LANES = 128
QK_PAD = 2 * LANES
V_EXTRA_ROWS = 16
HALO = 8
NEG = -1e30

Q_SCALE = QK_HEAD_DIM ** -0.5 * math.log2(math.e)
MAX_FIXED_REF_SPAN = 100.0
BOUND_MARGIN = 1.02

TOKEN_TILE = 512
FF_CHUNK = 1024
Q_TILE = 1024
KV_STEP = 1024
DIAG_TILE = 256
FALLBACK_KV_TILE = 512
VMEM_LIMIT = 56 * 1024 * 1024

F32 = jnp.float32
BF16 = jnp.bfloat16


def _rms(x):
    return x * lax.rsqrt(jnp.mean(x * x, axis=-1, keepdims=True) + EPS)


def _dot(a, b):
    return jnp.dot(a, b, preferred_element_type=F32)


def _resident(shape):
    return pl.BlockSpec(shape, lambda *_: (0,) * len(shape), pipeline_mode=pl.Buffered(1))


def _resident_layer(stacked, layer):
    tail = stacked.shape[1:]
    return pl.BlockSpec((None,) + tail, lambda *_: (layer,) + (0,) * len(tail), pipeline_mode=pl.Buffered(1))


def _mlp_tail(x1, g_ref, w1_ref, w2_ref):
    h = (_rms(x1) * g_ref[...]).astype(BF16)
    acc = x1
    for c in range(D_FF // FF_CHUNK):
        cols = slice(c * FF_CHUNK, (c + 1) * FF_CHUNK)
        u = jnp.maximum(_dot(h, w1_ref[:, cols]), 0.0)
        acc = acc + _dot((u * u).astype(BF16), w2_ref[cols, :])
    return acc


def _dot_nt(a, b):
    return lax.dot_general(a, b, (((1,), (1,)), ((), ())), preferred_element_type=F32)


def _qkv_kernel(x_ref, cos_ref, sin_ref, g_mix_ref, w_down_ref, g_qa_ref, g_kva_ref, w_q_ref, w_kn_ref, w_vt_ref,
                gq_n_ref, gq_r_ref, gq_rs_ref, gk_n_ref, gk_r_ref, gk_rs_ref, q_pad_ref, q_ref, k_ref, vt_ref,
                *, q_scale):
    x = x_ref[0]
    h = (_rms(x) * g_mix_ref[...]).astype(BF16)
    a = _dot(h, w_down_ref[...])
    c_q = (_rms(a[:, :Q_LORA]) * g_qa_ref[...]).astype(BF16)
    c_kv = (_rms(a[:, Q_LORA:Q_LORA + KV_LORA]) * g_kva_ref[...]).astype(BF16)
    kpe = a[:, Q_LORA + KV_LORA:Q_LORA + KV_LORA + LANES]
    kpe_sw = a[:, Q_LORA + KV_LORA + LANES:]
    qa = _dot(c_q, w_q_ref[...])
    kna = _dot(c_kv, w_kn_ref[...])
    vta = _dot_nt(w_vt_ref[...], c_kv)

    cos2 = cos_ref[0]
    sin2 = sin_ref[0]
    lane = lax.broadcasted_iota(jnp.int32, (1, LANES), 1)
    is_rope = lane < ROPE
    k_pad = jnp.where(lane == ROPE, 1.0, 0.0)
    inv_d = 1.0 / QK_HEAD_DIM

    k_rot = kpe * (gk_r_ref[...] * cos2) + kpe_sw * (gk_rs_ref[...] * sin2)
    ssq_kpe = jnp.sum(kpe * kpe, axis=-1, keepdims=True)
    q_cos = gq_r_ref[...] * cos2
    q_sin = gq_rs_ref[...] * sin2

    q_off = N_HEADS * NOPE
    for hd in range(N_HEADS):
        qn = qa[:, hd * NOPE:(hd + 1) * NOPE]
        qr = qa[:, q_off + hd * LANES:q_off + (hd + 1) * LANES]
        qr_sw = qa[:, q_off + (N_HEADS + hd) * LANES:q_off + (N_HEADS + hd + 1) * LANES]
        ssq = jnp.sum(qn * qn, axis=-1, keepdims=True) + jnp.sum(qr * qr, axis=-1, keepdims=True)
        r = lax.rsqrt(ssq * inv_d + EPS) * q_scale
        q_ref[0, hd, :, :NOPE] = (qn * r * gq_n_ref[...]).astype(BF16)
        q_rot = (qr * q_cos + qr_sw * q_sin) * r
        q_ref[0, hd, :, NOPE:] = jnp.where(is_rope, q_rot, q_pad_ref[...]).astype(BF16)

        kn = kna[:, hd * NOPE:(hd + 1) * NOPE]
        r = lax.rsqrt((jnp.sum(kn * kn, axis=-1, keepdims=True) + ssq_kpe) * inv_d + EPS)
        k_ref[0, hd, :, :NOPE] = (kn * r * gk_n_ref[...]).astype(BF16)
        k_ref[0, hd, :, NOPE:] = jnp.where(is_rope, k_rot * r, k_pad).astype(BF16)
        vt_ref[0, hd] = vta[hd * V_DIM:(hd + 1) * V_DIM, :].astype(BF16)


def _qkv_call(x, cos2, sin2, g_mix, w):
    b, s, _ = x.shape
    tm = TOKEN_TILE
    row = lambda n: _resident((1, n))
    out_shape = (jax.ShapeDtypeStruct((b, N_HEADS, s, QK_PAD), BF16),
                 jax.ShapeDtypeStruct((b, N_HEADS, s, QK_PAD), BF16),
                 jax.ShapeDtypeStruct((b, N_HEADS, V_DIM, s), BF16))
    return pl.pallas_call(
        functools.partial(_qkv_kernel, q_scale=Q_SCALE),
        grid=(b, s // tm),
        in_specs=[
            pl.BlockSpec((1, tm, D_MODEL), lambda bi, i: (bi, i, 0)),
            pl.BlockSpec((1, tm, LANES), lambda bi, i: (bi, i, 0)),
            pl.BlockSpec((1, tm, LANES), lambda bi, i: (bi, i, 0)),
            row(D_MODEL),
            _resident(w['w_down'].shape), row(Q_LORA), row(KV_LORA),
            _resident(w['w_q'].shape), _resident(w['w_kn'].shape), _resident(w['w_vt'].shape),
            row(NOPE), row(LANES), row(LANES), row(NOPE), row(LANES), row(LANES), row(LANES),
        ],
        out_specs=(
            pl.BlockSpec((1, N_HEADS, tm, QK_PAD), lambda bi, i: (bi, 0, i, 0)),
            pl.BlockSpec((1, N_HEADS, tm, QK_PAD), lambda bi, i: (bi, 0, i, 0)),
            pl.BlockSpec((1, N_HEADS, V_DIM, tm), lambda bi, i: (bi, 0, 0, i)),
        ),
        out_shape=out_shape,
        compiler_params=pltpu.CompilerParams(
            dimension_semantics=("parallel", "parallel"), vmem_limit_bytes=VMEM_LIMIT),
        name="qkv_proj",
    )(x, cos2, sin2, g_mix, w['w_down'], w['g_qa'], w['g_kva'], w['w_q'], w['w_kn'], w['w_vt'],
      w['gq_n'], w['gq_r'], w['gq_rs'], w['gk_n'], w['gk_r'], w['gk_rs'], w['q_pad'])


def _flash_fixed_ref(q_ref, k_ref, vt_ref, o_ref, acc_t_ref):
    i = pl.program_id(2)
    tq, step, dg = Q_TILE, KV_STEP, DIAG_TILE
    ones_row = jnp.where(lax.broadcasted_iota(jnp.int32, (V_EXTRA_ROWS, 1), 0) == 0, 1.0, 0.0).astype(BF16)

    def weighted_values_t(p_t, off, n):
        v_t = jnp.concatenate([vt_ref[0, 0, :, pl.ds(off, n)], jnp.broadcast_to(ones_row, (V_EXTRA_ROWS, n))], axis=0)
        return _dot(v_t, p_t.astype(BF16))

    acc_t_ref[...] = jnp.zeros(acc_t_ref.shape, F32)

    def full_step(j, carry):
        off = pl.multiple_of(j * step, step)
        s_t = _dot_nt(k_ref[0, 0, pl.ds(off, step), :], q_ref[0, 0])
        acc_t_ref[...] += weighted_values_t(jnp.exp2(s_t), off, step)
        return carry

    lax.fori_loop(0, i * (tq // step), full_step, 0)

    base = pl.multiple_of(i * tq, tq)
    for r in range(tq // dg):
        cols = slice(r * dg, (r + 1) * dg)
        n = (r + 1) * dg
        s_t = _dot_nt(k_ref[0, 0, pl.ds(base, n), :], q_ref[0, 0, cols, :])
        visible = (lax.broadcasted_iota(jnp.int32, (n, dg), 0)
                   <= lax.broadcasted_iota(jnp.int32, (n, dg), 1) + r * dg)
        acc_t_ref[:, cols] += weighted_values_t(jnp.exp2(jnp.where(visible, s_t, NEG)), base, n)

    acc_t = acc_t_ref[...]
    o_ref[0] = jnp.transpose(acc_t[:V_DIM, :] / acc_t[V_DIM:V_DIM + 1, :]).astype(BF16)


def _flash_running_max(q_ref, k_ref, vt_ref, o_ref, m_ref, l_ref, acc_ref):
    i = pl.program_id(2)
    tq, tk = Q_TILE, FALLBACK_KV_TILE
    m_ref[...] = jnp.full(m_ref.shape, NEG, F32)
    l_ref[...] = jnp.zeros(l_ref.shape, F32)
    acc_ref[...] = jnp.zeros(acc_ref.shape, F32)

    def step(off, mask):
        s = _dot_nt(q_ref[0, 0], k_ref[0, 0, pl.ds(off, tk), :])
        if mask is not None:
            s = jnp.where(mask, s, NEG)
        m_old = m_ref[...]
        m_new = jnp.maximum(m_old, jnp.max(s, axis=-1, keepdims=True))
        alpha = jnp.exp2(m_old - m_new)
        p = jnp.exp2(s - m_new)
        l_ref[...] = alpha * l_ref[...] + jnp.sum(p, axis=-1, keepdims=True)
        acc_ref[...] = alpha * acc_ref[...] + _dot_nt(p.astype(BF16), vt_ref[0, 0, :, pl.ds(off, tk)])
        m_ref[...] = m_new

    def full_step(j, carry):
        step(pl.multiple_of(j * tk, tk), None)
        return carry

    lax.fori_loop(0, i * (tq // tk), full_step, 0)
    row = lax.broadcasted_iota(jnp.int32, (tq, tk), 0)
    col = lax.broadcasted_iota(jnp.int32, (tq, tk), 1)
    for d in range(tq // tk):
        step(pl.multiple_of(i * tq + d * tk, tk), row >= col + d * tk)
    o_ref[0] = (acc_ref[...] / l_ref[...]).astype(BF16)


def _flash_kernel(fixed_ref_ok, q_ref, k_ref, vt_ref, o_ref, acc_t_ref, m_ref, l_ref, acc_ref):
    @pl.when(fixed_ref_ok[0] == 1)
    def _():
        _flash_fixed_ref(q_ref, k_ref, vt_ref, o_ref, acc_t_ref)

    @pl.when(fixed_ref_ok[0] == 0)
    def _():
        _flash_running_max(q_ref, k_ref, vt_ref, o_ref, m_ref, l_ref, acc_ref)


def _flash_call(fixed_ref_ok, q, k, v_t):
    b, nh, s, _ = q.shape
    tq = Q_TILE
    grid_spec = pltpu.PrefetchScalarGridSpec(
        num_scalar_prefetch=1,
        grid=(b, nh, s // tq),
        in_specs=[
            pl.BlockSpec((1, 1, tq, QK_PAD), lambda bi, h, i, _: (bi, h, i, 0)),
            pl.BlockSpec((1, 1, s, QK_PAD), lambda bi, h, i, _: (bi, h, 0, 0)),
            pl.BlockSpec((1, 1, V_DIM, s), lambda bi, h, i, _: (bi, h, 0, 0)),
        ],
        out_specs=pl.BlockSpec((1, tq, V_DIM), lambda bi, h, i, _: (bi, i, h)),
        scratch_shapes=[pltpu.VMEM((V_DIM + V_EXTRA_ROWS, tq), F32),
                        pltpu.VMEM((tq, 1), F32), pltpu.VMEM((tq, 1), F32), pltpu.VMEM((tq, V_DIM), F32)],
    )
    return pl.pallas_call(
        _flash_kernel,
        grid_spec=grid_spec,
        out_shape=jax.ShapeDtypeStruct((b, s, nh * V_DIM), BF16),
        compiler_params=pltpu.CompilerParams(
            dimension_semantics=("parallel", "parallel", "arbitrary"), vmem_limit_bytes=VMEM_LIMIT),
        name="flash_attn",
    )(fixed_ref_ok, q, k, v_t)


def _attn_out_mlp_kernel(x_ref, o_ref, w_o_ref, g_ref, w1_ref, w2_ref, out_ref):
    x1 = x_ref[...] + _dot(o_ref[...], w_o_ref[...])
    out_ref[...] = _mlp_tail(x1, g_ref, w1_ref, w2_ref)


def _attn_out_mlp_call(x2d, o2d, w_o, g_mlp, w1, w2, attn_layer, layer):
    t, _ = x2d.shape
    tm = TOKEN_TILE
    return pl.pallas_call(
        _attn_out_mlp_kernel,
        grid=(t // tm,),
        in_specs=[
            pl.BlockSpec((tm, D_MODEL), lambda i: (i, 0)),
            pl.BlockSpec((tm, N_HEADS * V_DIM), lambda i: (i, 0)),
            _resident_layer(w_o, attn_layer), _resident((1, D_MODEL)),
            _resident_layer(w1, layer), _resident_layer(w2, layer),
        ],
        out_specs=pl.BlockSpec((tm, D_MODEL), lambda i: (i, 0)),
        out_shape=jax.ShapeDtypeStruct((t, D_MODEL), F32),
        compiler_params=pltpu.CompilerParams(dimension_semantics=("parallel",), vmem_limit_bytes=VMEM_LIMIT),
        name="attn_out_mlp",
    )(x2d, o2d, w_o, g_mlp, w1, w2)


def _conv_mlp_kernel(x_ref, g_mix_ref, w_in_ref, cw_ref, w_out_ref, g_ref, w1_ref, w2_ref, out_ref, u_buf):
    tm = x_ref.shape[1]

    @pl.when(pl.program_id(1) == 0)
    def _():
        u_buf[:HALO, :] = jnp.zeros((HALO, D_MODEL), F32)

    x = x_ref[0]
    h = (_rms(x) * g_mix_ref[...]).astype(BF16)
    gate_c = _dot(h, w_in_ref[:, D_MODEL:2 * D_MODEL])
    u = gate_c * _dot(h, w_in_ref[:, 2 * D_MODEL:])
    u_buf[HALO:, :] = u
    conv = (u_buf[HALO - 2:HALO - 2 + tm, :] * cw_ref[0:1, :]
            + u_buf[HALO - 1:HALO - 1 + tm, :] * cw_ref[1:2, :]
            + u * cw_ref[2:3, :])
    u_buf[:HALO, :] = u_buf[tm:tm + HALO, :]
    gate_b = _dot(h, w_in_ref[:, :D_MODEL])
    x1 = x + _dot((gate_b * conv).astype(BF16), w_out_ref[...])
    out_ref[0] = _mlp_tail(x1, g_ref, w1_ref, w2_ref)


def _conv_mlp_call(x, g_mix, w_in, conv_w, w_out, g_mlp, w1, w2, conv_layer, layer):
    b, s, _ = x.shape
    tm = TOKEN_TILE
    return pl.pallas_call(
        _conv_mlp_kernel,
        grid=(b, s // tm),
        in_specs=[
            pl.BlockSpec((1, tm, D_MODEL), lambda bi, i: (bi, i, 0)),
            _resident((1, D_MODEL)), _resident_layer(w_in, conv_layer), _resident_layer(conv_w, conv_layer),
            _resident_layer(w_out, conv_layer),
            _resident((1, D_MODEL)), _resident_layer(w1, layer), _resident_layer(w2, layer),
        ],
        out_specs=pl.BlockSpec((1, tm, D_MODEL), lambda bi, i: (bi, i, 0)),
        out_shape=jax.ShapeDtypeStruct((b, s, D_MODEL), F32),
        scratch_shapes=[pltpu.VMEM((tm + HALO, D_MODEL), F32)],
        compiler_params=pltpu.CompilerParams(
            dimension_semantics=("parallel", "arbitrary"), vmem_limit_bytes=VMEM_LIMIT),
        name="conv_mlp",
    )(x, g_mix, w_in, conv_w, w_out, g_mlp, w1, w2)


def _swap_halves(w):
    half = w.shape[-1] // 2
    return jnp.concatenate([w[..., half:], w[..., :half]], axis=-1)


def _attn_params(w_down, g_q_a, g_kv_a, w_uq, w_ukv, g_qnorm, g_knorm):
    pad = lambda w: jnp.pad(w, [(0, 0)] * (w.ndim - 1) + [(0, LANES - w.shape[-1])])
    kpe = w_down[:, Q_LORA + KV_LORA:]
    w_down_ext = jnp.concatenate([w_down[:, :Q_LORA + KV_LORA], pad(kpe), pad(_swap_halves(kpe))], axis=1)
    uq = w_uq.reshape(Q_LORA, N_HEADS, QK_HEAD_DIM)
    uq_r = uq[:, :, NOPE:]
    w_q = jnp.concatenate([uq[:, :, :NOPE].reshape(Q_LORA, -1), pad(uq_r).reshape(Q_LORA, -1),
                           pad(_swap_halves(uq_r)).reshape(Q_LORA, -1)], axis=1)
    ukv = w_ukv.reshape(KV_LORA, N_HEADS, NOPE + V_DIM)
    w_kn = ukv[:, :, :NOPE].reshape(KV_LORA, -1)
    w_vt = ukv[:, :, NOPE:].reshape(KV_LORA, -1).T
    row = lambda g: pad(g)[None, :]
    bound = (BOUND_MARGIN * Q_SCALE * QK_HEAD_DIM) * jnp.max(jnp.abs(g_qnorm)) * jnp.max(jnp.abs(g_knorm))
    fixed_ref_ok = (2.0 * bound < MAX_FIXED_REF_SPAN).astype(jnp.int32).reshape(1)
    q_pad = jnp.where(jnp.arange(LANES) == ROPE, -bound, 0.0)[None, :]
    return {
        'w_down': w_down_ext.astype(BF16), 'g_qa': g_q_a[None, :], 'g_kva': g_kv_a[None, :],
        'w_q': w_q.astype(BF16), 'w_kn': w_kn.astype(BF16), 'w_vt': w_vt.astype(BF16),
        'gq_n': g_qnorm[None, :NOPE], 'gq_r': row(g_qnorm[NOPE:]), 'gq_rs': row(_swap_halves(g_qnorm[NOPE:])),
        'gk_n': g_knorm[None, :NOPE], 'gk_r': row(g_knorm[NOPE:]), 'gk_rs': row(_swap_halves(g_knorm[NOPE:])),
        'q_pad': q_pad, 'fixed_ref_ok': fixed_ref_ok,
    }


def _rope_tables(positions):
    inv_freq = ROPE_THETA ** (-jnp.arange(0, ROPE, 2, dtype=F32) / ROPE)
    ang = positions.astype(F32)[..., None] * inv_freq
    cos, sin = jnp.cos(ang), jnp.sin(ang)
    return jnp.concatenate([cos] * 4, axis=-1), jnp.concatenate([-sin, sin] * 2, axis=-1)


def kernel(x, positions, g_mix, g_mlp, attn_w_down, attn_g_q_a, attn_g_kv_a, attn_w_uq, attn_w_ukv, attn_g_qnorm,
           attn_g_knorm, attn_w_o, conv_w_in, conv_w, conv_w_out, mlp_w1, mlp_w2):
    b, s, d = x.shape
    depth = g_mix.shape[0]
    cos2, sin2 = _rope_tables(positions)
    w1, w2 = mlp_w1.astype(BF16), mlp_w2.astype(BF16)
    w_o, w_in, w_out = attn_w_o.astype(BF16), conv_w_in.astype(BF16), conv_w_out.astype(BF16)
    for i in range(depth):
        j = i // 2
        if i % 2 == 0:
            w = _attn_params(attn_w_down[j], attn_g_q_a[j], attn_g_kv_a[j], attn_w_uq[j], attn_w_ukv[j],
                             attn_g_qnorm[j], attn_g_knorm[j])
            q, k, v_t = _qkv_call(x, cos2, sin2, g_mix[i][None, :], w)
            o = _flash_call(w['fixed_ref_ok'], q, k, v_t)
            x = _attn_out_mlp_call(x.reshape(b * s, d), o.reshape(b * s, d), w_o, g_mlp[i][None, :], w1, w2,
                                   j, i).reshape(b, s, d)
        else:
            x = _conv_mlp_call(x, g_mix[i][None, :], w_in, conv_w, w_out, g_mlp[i][None, :], w1, w2, j, i)
    return x
```

```python
import functools
import math

import jax
import jax.numpy as jnp
from jax import lax
from jax.experimental import pallas as pl
from jax.experimental.pallas import tpu as pltpu

D_MODEL = 1024
N_HEADS = 8
NOPE = 128
ROPE = 64
QK_HEAD_DIM = NOPE + ROPE
V_DIM = 128
Q_LORA = 256
KV_LORA = 128
D_FF = 4096
CONV_WIDTH = 3
EPS = 1e-6
ROPE_THETA = 10000.0

LANES = 128
QK_PAD = 2 * LANES
V_EXTRA_ROWS = 16
HALO = 8
NEG = -1e30

Q_SCALE = QK_HEAD_DIM ** -0.5 * math.log2(math.e)
MAX_FIXED_REF_SPAN = 100.0
BOUND_MARGIN = 1.02

TOKEN_TILE = 512
FF_CHUNK = 1024
Q_TILE = 2048
KV_STEP = 1024
DIAG_TILE = 512
FALLBACK_KV_TILE = 512
VMEM_LIMIT = 56 * 1024 * 1024

F32 = jnp.float32
BF16 = jnp.bfloat16


def _rms(x):
    return x * lax.rsqrt(jnp.mean(x * x, axis=-1, keepdims=True) + EPS)


def _dot(a, b):
    return jnp.dot(a, b, preferred_element_type=F32)


def _resident(shape):
    return pl.BlockSpec(shape, lambda *_: (0,) * len(shape), pipeline_mode=pl.Buffered(1))


def _resident_layer(stacked, layer):
    tail = stacked.shape[1:]
    return pl.BlockSpec((None,) + tail, lambda *_: (layer,) + (0,) * len(tail), pipeline_mode=pl.Buffered(1))


def _mlp_tail(x1, g_ref, w1_ref, w2_ref):
    h = (_rms(x1) * g_ref[...]).astype(BF16)
    acc = x1
    for c in range(D_FF // FF_CHUNK):
        cols = slice(c * FF_CHUNK, (c + 1) * FF_CHUNK)
        u = jnp.maximum(_dot(h, w1_ref[:, cols]), 0.0)
        acc = acc + _dot((u * u).astype(BF16), w2_ref[cols, :])
    return acc


def _dot_nt(a, b):
    return lax.dot_general(a, b, (((1,), (1,)), ((), ())), preferred_element_type=F32)


def _qkv_kernel(x_ref, cos_ref, sin_ref, g_mix_ref, w_down_ref, g_qa_ref, g_kva_ref, w_q_ref, w_kn_ref, w_vt_ref,
                gq_n_ref, gq_r_ref, gq_rs_ref, gk_n_ref, gk_r_ref, gk_rs_ref, q_pad_ref, q_ref, k_ref, vt_ref,
                *, q_scale):
    x = x_ref[0]
    h = (_rms(x) * g_mix_ref[...]).astype(BF16)
    a = _dot(h, w_down_ref[...])
    c_q = (_rms(a[:, :Q_LORA]) * g_qa_ref[...]).astype(BF16)
    c_kv = (_rms(a[:, Q_LORA:Q_LORA + KV_LORA]) * g_kva_ref[...]).astype(BF16)
    kpe = a[:, Q_LORA + KV_LORA:Q_LORA + KV_LORA + LANES]
    kpe_sw = a[:, Q_LORA + KV_LORA + LANES:]
    qa = _dot(c_q, w_q_ref[...])
    kna = _dot(c_kv, w_kn_ref[...])
    vta = _dot_nt(w_vt_ref[...], c_kv)

    cos2 = jnp.transpose(cos_ref[0])
    sin2 = jnp.transpose(sin_ref[0])
    lane = lax.broadcasted_iota(jnp.int32, (1, LANES), 1)
    is_rope = lane < ROPE
    k_pad = jnp.where(lane == ROPE, 1.0, 0.0)
    inv_d = 1.0 / QK_HEAD_DIM

    k_rot = kpe * (gk_r_ref[...] * cos2) + kpe_sw * (gk_rs_ref[...] * sin2)
    ssq_kpe = jnp.sum(kpe * kpe, axis=-1, keepdims=True)
    q_cos = gq_r_ref[...] * cos2
    q_sin = gq_rs_ref[...] * sin2

    q_off = N_HEADS * NOPE
    for hd in range(N_HEADS):
        qn = qa[:, hd * NOPE:(hd + 1) * NOPE]
        qr = qa[:, q_off + hd * LANES:q_off + (hd + 1) * LANES]
        qr_sw = qa[:, q_off + (N_HEADS + hd) * LANES:q_off + (N_HEADS + hd + 1) * LANES]
        ssq = jnp.sum(qn * qn, axis=-1, keepdims=True) + jnp.sum(qr * qr, axis=-1, keepdims=True)
        r = lax.rsqrt(ssq * inv_d + EPS) * q_scale
        q_ref[0, hd, :, :NOPE] = (qn * r * gq_n_ref[...]).astype(BF16)
        q_rot = (qr * q_cos + qr_sw * q_sin) * r
        q_ref[0, hd, :, NOPE:] = jnp.where(is_rope, q_rot, q_pad_ref[...]).astype(BF16)

        kn = kna[:, hd * NOPE:(hd + 1) * NOPE]
        r = lax.rsqrt((jnp.sum(kn * kn, axis=-1, keepdims=True) + ssq_kpe) * inv_d + EPS)
        k_ref[0, hd, :, :NOPE] = (kn * r * gk_n_ref[...]).astype(BF16)
        k_ref[0, hd, :, NOPE:] = jnp.where(is_rope, k_rot * r, k_pad).astype(BF16)
        vt_ref[0, hd] = vta[hd * V_DIM:(hd + 1) * V_DIM, :].astype(BF16)


def _qkv_call(x, cos2, sin2, g_mix, w):
    b, s, _ = x.shape
    tm = TOKEN_TILE
    row = lambda n: _resident((1, n))
    out_shape = (jax.ShapeDtypeStruct((b, N_HEADS, s, QK_PAD), BF16),
                 jax.ShapeDtypeStruct((b, N_HEADS, s, QK_PAD), BF16),
                 jax.ShapeDtypeStruct((b, N_HEADS, V_DIM, s), BF16))
    return pl.pallas_call(
        functools.partial(_qkv_kernel, q_scale=Q_SCALE),
        grid=(b, s // tm),
        in_specs=[
            pl.BlockSpec((1, tm, D_MODEL), lambda bi, i: (bi, i, 0)),
            pl.BlockSpec((1, LANES, tm), lambda bi, i: (bi, 0, i)),
            pl.BlockSpec((1, LANES, tm), lambda bi, i: (bi, 0, i)),
            row(D_MODEL),
            _resident(w['w_down'].shape), row(Q_LORA), row(KV_LORA),
            _resident(w['w_q'].shape), _resident(w['w_kn'].shape), _resident(w['w_vt'].shape),
            row(NOPE), row(LANES), row(LANES), row(NOPE), row(LANES), row(LANES), row(LANES),
        ],
        out_specs=(
            pl.BlockSpec((1, N_HEADS, tm, QK_PAD), lambda bi, i: (bi, 0, i, 0)),
            pl.BlockSpec((1, N_HEADS, tm, QK_PAD), lambda bi, i: (bi, 0, i, 0)),
            pl.BlockSpec((1, N_HEADS, V_DIM, tm), lambda bi, i: (bi, 0, 0, i)),
        ),
        out_shape=out_shape,
        compiler_params=pltpu.CompilerParams(
            dimension_semantics=("parallel", "parallel"), vmem_limit_bytes=VMEM_LIMIT),
        name="qkv_proj",
    )(x, cos2, sin2, g_mix, w['w_down'], w['g_qa'], w['g_kva'], w['w_q'], w['w_kn'], w['w_vt'],
      w['gq_n'], w['gq_r'], w['gq_rs'], w['gk_n'], w['gk_r'], w['gk_rs'], w['q_pad'])


def _flash_fixed_ref(q_ref, k_ref, vt_ref, o_ref, acc_t_ref):
    i = pl.program_id(2)
    tq, step, dg = Q_TILE, KV_STEP, DIAG_TILE
    ones_row = jnp.where(lax.broadcasted_iota(jnp.int32, (V_EXTRA_ROWS, 1), 0) == 0, 1.0, 0.0).astype(BF16)

    def weighted_values_t(p_t, off, n):
        v_t = jnp.concatenate([vt_ref[0, 0, :, pl.ds(off, n)], jnp.broadcast_to(ones_row, (V_EXTRA_ROWS, n))], axis=0)
        return _dot(v_t, p_t.astype(BF16))

    acc_t_ref[...] = jnp.zeros(acc_t_ref.shape, F32)

    def full_step(j, carry):
        off = pl.multiple_of(j * step, step)
        s_t = _dot_nt(k_ref[0, 0, pl.ds(off, step), :], q_ref[0, 0])
        acc_t_ref[...] += weighted_values_t(jnp.exp2(s_t), off, step)
        return carry

    lax.fori_loop(0, i * (tq // step), full_step, 0)

    base = pl.multiple_of(i * tq, tq)
    for r in range(tq // dg):
        cols = slice(r * dg, (r + 1) * dg)
        n = (r + 1) * dg
        s_t = _dot_nt(k_ref[0, 0, pl.ds(base, n), :], q_ref[0, 0, cols, :])
        visible = (lax.broadcasted_iota(jnp.int32, (n, dg), 0)
                   <= lax.broadcasted_iota(jnp.int32, (n, dg), 1) + r * dg)
        acc_t = acc_t_ref[:, cols] + weighted_values_t(jnp.exp2(jnp.where(visible, s_t, NEG)), base, n)
        o_ref[0, cols, :] = jnp.transpose(acc_t[:V_DIM, :] / acc_t[V_DIM:V_DIM + 1, :]).astype(BF16)


def _flash_running_max(q_ref, k_ref, vt_ref, o_ref, m_ref, l_ref, acc_ref):
    i = pl.program_id(2)
    tq, tk = Q_TILE, FALLBACK_KV_TILE
    m_ref[...] = jnp.full(m_ref.shape, NEG, F32)
    l_ref[...] = jnp.zeros(l_ref.shape, F32)
    acc_ref[...] = jnp.zeros(acc_ref.shape, F32)

    def step(off, mask):
        s = _dot_nt(q_ref[0, 0], k_ref[0, 0, pl.ds(off, tk), :])
        if mask is not None:
            s = jnp.where(mask, s, NEG)
        m_old = m_ref[...]
        m_new = jnp.maximum(m_old, jnp.max(s, axis=-1, keepdims=True))
        alpha = jnp.exp2(m_old - m_new)
        p = jnp.exp2(s - m_new)
        l_ref[...] = alpha * l_ref[...] + jnp.sum(p, axis=-1, keepdims=True)
        acc_ref[...] = alpha * acc_ref[...] + _dot_nt(p.astype(BF16), vt_ref[0, 0, :, pl.ds(off, tk)])
        m_ref[...] = m_new

    def full_step(j, carry):
        step(pl.multiple_of(j * tk, tk), None)
        return carry

    lax.fori_loop(0, i * (tq // tk), full_step, 0)
    row = lax.broadcasted_iota(jnp.int32, (tq, tk), 0)
    col = lax.broadcasted_iota(jnp.int32, (tq, tk), 1)
    for d in range(tq // tk):
        step(pl.multiple_of(i * tq + d * tk, tk), row >= col + d * tk)
    o_ref[0] = (acc_ref[...] / l_ref[...]).astype(BF16)


def _flash_kernel(fixed_ref_ok, q_ref, k_ref, vt_ref, o_ref, acc_t_ref, m_ref, l_ref, acc_ref):
    @pl.when(fixed_ref_ok[0] == 1)
    def _():
        _flash_fixed_ref(q_ref, k_ref, vt_ref, o_ref, acc_t_ref)

    @pl.when(fixed_ref_ok[0] == 0)
    def _():
        _flash_running_max(q_ref, k_ref, vt_ref, o_ref, m_ref, l_ref, acc_ref)


def _flash_call(fixed_ref_ok, q, k, v_t):
    b, nh, s, _ = q.shape
    tq = Q_TILE
    grid_spec = pltpu.PrefetchScalarGridSpec(
        num_scalar_prefetch=1,
        grid=(b, nh, s // tq),
        in_specs=[
            pl.BlockSpec((1, 1, tq, QK_PAD), lambda bi, h, i, _: (bi, h, i, 0)),
            pl.BlockSpec((1, 1, s, QK_PAD), lambda bi, h, i, _: (bi, h, 0, 0)),
            pl.BlockSpec((1, 1, V_DIM, s), lambda bi, h, i, _: (bi, h, 0, 0)),
        ],
        out_specs=pl.BlockSpec((1, tq, V_DIM), lambda bi, h, i, _: (bi, i, h)),
        scratch_shapes=[pltpu.VMEM((V_DIM + V_EXTRA_ROWS, tq), F32),
                        pltpu.VMEM((tq, 1), F32), pltpu.VMEM((tq, 1), F32), pltpu.VMEM((tq, V_DIM), F32)],
    )
    return pl.pallas_call(
        _flash_kernel,
        grid_spec=grid_spec,
        out_shape=jax.ShapeDtypeStruct((b, s, nh * V_DIM), BF16),
        compiler_params=pltpu.CompilerParams(
            dimension_semantics=("parallel", "parallel", "arbitrary"), vmem_limit_bytes=VMEM_LIMIT),
        name="flash_attn",
    )(fixed_ref_ok, q, k, v_t)


def _attn_out_mlp_kernel(x_ref, o_ref, w_o_ref, g_ref, w1_ref, w2_ref, out_ref):
    x1 = x_ref[...] + _dot(o_ref[...], w_o_ref[...])
    out_ref[...] = _mlp_tail(x1, g_ref, w1_ref, w2_ref)


def _attn_out_mlp_call(x2d, o2d, w_o, g_mlp, w1, w2, attn_layer, layer):
    t, _ = x2d.shape
    tm = TOKEN_TILE
    return pl.pallas_call(
        _attn_out_mlp_kernel,
        grid=(t // tm,),
        in_specs=[
            pl.BlockSpec((tm, D_MODEL), lambda i: (i, 0)),
            pl.BlockSpec((tm, N_HEADS * V_DIM), lambda i: (i, 0)),
            _resident_layer(w_o, attn_layer), _resident((1, D_MODEL)),
            _resident_layer(w1, layer), _resident_layer(w2, layer),
        ],
        out_specs=pl.BlockSpec((tm, D_MODEL), lambda i: (i, 0)),
        out_shape=jax.ShapeDtypeStruct((t, D_MODEL), F32),
        compiler_params=pltpu.CompilerParams(dimension_semantics=("parallel",), vmem_limit_bytes=VMEM_LIMIT),
        name="attn_out_mlp",
    )(x2d, o2d, w_o, g_mlp, w1, w2)


def _conv_mlp_kernel(x_ref, g_mix_ref, w_in_ref, cw_ref, w_out_ref, g_ref, w1_ref, w2_ref, out_ref, u_buf):
    tm = x_ref.shape[1]

    @pl.when(pl.program_id(1) == 0)
    def _():
        u_buf[:HALO, :] = jnp.zeros((HALO, D_MODEL), F32)

    x = x_ref[0]
    h = (_rms(x) * g_mix_ref[...]).astype(BF16)
    gate_c = _dot(h, w_in_ref[:, D_MODEL:2 * D_MODEL])
    u = gate_c * _dot(h, w_in_ref[:, 2 * D_MODEL:])
    u_buf[HALO:, :] = u
    conv = (u_buf[HALO - 2:HALO - 2 + tm, :] * cw_ref[0:1, :]
            + u_buf[HALO - 1:HALO - 1 + tm, :] * cw_ref[1:2, :]
            + u * cw_ref[2:3, :])
    u_buf[:HALO, :] = u_buf[tm:tm + HALO, :]
    gate_b = _dot(h, w_in_ref[:, :D_MODEL])
    x1 = x + _dot((gate_b * conv).astype(BF16), w_out_ref[...])
    out_ref[0] = _mlp_tail(x1, g_ref, w1_ref, w2_ref)


def _conv_mlp_call(x, g_mix, w_in, conv_w, w_out, g_mlp, w1, w2, conv_layer, layer):
    b, s, _ = x.shape
    tm = TOKEN_TILE
    return pl.pallas_call(
        _conv_mlp_kernel,
        grid=(b, s // tm),
        in_specs=[
            pl.BlockSpec((1, tm, D_MODEL), lambda bi, i: (bi, i, 0)),
            _resident((1, D_MODEL)), _resident_layer(w_in, conv_layer), _resident_layer(conv_w, conv_layer),
            _resident_layer(w_out, conv_layer),
            _resident((1, D_MODEL)), _resident_layer(w1, layer), _resident_layer(w2, layer),
        ],
        out_specs=pl.BlockSpec((1, tm, D_MODEL), lambda bi, i: (bi, i, 0)),
        out_shape=jax.ShapeDtypeStruct((b, s, D_MODEL), F32),
        scratch_shapes=[pltpu.VMEM((tm + HALO, D_MODEL), F32)],
        compiler_params=pltpu.CompilerParams(
            dimension_semantics=("parallel", "arbitrary"), vmem_limit_bytes=VMEM_LIMIT),
        name="conv_mlp",
    )(x, g_mix, w_in, conv_w, w_out, g_mlp, w1, w2)


def _swap_halves(w):
    half = w.shape[-1] // 2
    return jnp.concatenate([w[..., half:], w[..., :half]], axis=-1)


def _attn_params(w_down, g_q_a, g_kv_a, w_uq, w_ukv, g_qnorm, g_knorm):
    pad = lambda w: jnp.pad(w, [(0, 0)] * (w.ndim - 1) + [(0, LANES - w.shape[-1])])
    kpe = w_down[:, Q_LORA + KV_LORA:]
    w_down_ext = jnp.concatenate([w_down[:, :Q_LORA + KV_LORA], pad(kpe), pad(_swap_halves(kpe))], axis=1)
    uq = w_uq.reshape(Q_LORA, N_HEADS, QK_HEAD_DIM)
    uq_r = uq[:, :, NOPE:]
    w_q = jnp.concatenate([uq[:, :, :NOPE].reshape(Q_LORA, -1), pad(uq_r).reshape(Q_LORA, -1),
                           pad(_swap_halves(uq_r)).reshape(Q_LORA, -1)], axis=1)
    ukv = w_ukv.reshape(KV_LORA, N_HEADS, NOPE + V_DIM)
    w_kn = ukv[:, :, :NOPE].reshape(KV_LORA, -1)
    w_vt = ukv[:, :, NOPE:].reshape(KV_LORA, -1).T
    row = lambda g: pad(g)[None, :]
    bound = (BOUND_MARGIN * Q_SCALE * QK_HEAD_DIM) * jnp.max(jnp.abs(g_qnorm)) * jnp.max(jnp.abs(g_knorm))
    fixed_ref_ok = (2.0 * bound < MAX_FIXED_REF_SPAN).astype(jnp.int32).reshape(1)
    q_pad = jnp.where(jnp.arange(LANES) == ROPE, -bound, 0.0)[None, :]
    return {
        'w_down': w_down_ext.astype(BF16), 'g_qa': g_q_a[None, :], 'g_kva': g_kv_a[None, :],
        'w_q': w_q.astype(BF16), 'w_kn': w_kn.astype(BF16), 'w_vt': w_vt.astype(BF16),
        'gq_n': g_qnorm[None, :NOPE], 'gq_r': row(g_qnorm[NOPE:]), 'gq_rs': row(_swap_halves(g_qnorm[NOPE:])),
        'gk_n': g_knorm[None, :NOPE], 'gk_r': row(g_knorm[NOPE:]), 'gk_rs': row(_swap_halves(g_knorm[NOPE:])),
        'q_pad': q_pad, 'fixed_ref_ok': fixed_ref_ok,
    }


def _rope_tables(positions):
    inv_freq = ROPE_THETA ** (-jnp.arange(0, ROPE, 2, dtype=F32) / ROPE)
    ang = positions.astype(F32)[:, None, :] * inv_freq[None, :, None]
    cos, sin = jnp.cos(ang), jnp.sin(ang)
    zeros = jnp.zeros((positions.shape[0], LANES - ROPE, positions.shape[1]), F32)
    return jnp.concatenate([cos, cos, zeros], axis=1), jnp.concatenate([-sin, sin, zeros], axis=1)


def kernel(x, positions, g_mix, g_mlp, attn_w_down, attn_g_q_a, attn_g_kv_a, attn_w_uq, attn_w_ukv, attn_g_qnorm,
           attn_g_knorm, attn_w_o, conv_w_in, conv_w, conv_w_out, mlp_w1, mlp_w2):
    b, s, d = x.shape
    depth = g_mix.shape[0]
    cos2, sin2 = _rope_tables(positions)
    w1, w2 = mlp_w1.astype(BF16), mlp_w2.astype(BF16)
    w_o, w_in, w_out = attn_w_o.astype(BF16), conv_w_in.astype(BF16), conv_w_out.astype(BF16)
    for i in range(depth):
        j = i // 2
        if i % 2 == 0:
            w = _attn_params(attn_w_down[j], attn_g_q_a[j], attn_g_kv_a[j], attn_w_uq[j], attn_w_ukv[j],
                             attn_g_qnorm[j], attn_g_knorm[j])
            q, k, v_t = _qkv_call(x, cos2, sin2, g_mix[i][None, :], w)
            o = _flash_call(w['fixed_ref_ok'], q, k, v_t)
            x = _attn_out_mlp_call(x.reshape(b * s, d), o.reshape(b * s, d), w_o, g_mlp[i][None, :], w1, w2,
                                   j, i).reshape(b, s, d)
        else:
            x = _conv_mlp_call(x, g_mix[i][None, :], w_in, conv_w, w_out, g_mlp[i][None, :], w1, w2, j, i)
    return x
```

```python
import functools
import math

import jax
import jax.numpy as jnp
from jax import lax
from jax.experimental import pallas as pl
from jax.experimental.pallas import tpu as pltpu

D_MODEL = 1024
N_HEADS = 8
NOPE = 128
ROPE = 64
QK_HEAD_DIM = NOPE + ROPE
V_DIM = 128
Q_LORA = 256
KV_LORA = 128
D_FF = 4096
CONV_WIDTH = 3
EPS = 1e-6
ROPE_THETA = 10000.0

LANES = 128
QK_PAD = 2 * LANES
V_EXTRA_ROWS = 16
HALO = 8
NEG = -1e30

Q_SCALE = QK_HEAD_DIM ** -0.5 * math.log2(math.e)
MAX_FIXED_REF_SPAN = 100.0
BOUND_MARGIN = 1.02

TOKEN_TILE = 512
QKV_TILE = 1024
QKV_SUB_TILE = 512
FF_CHUNK = 1024
Q_TILE = 2048
KV_STEP = 1024
DIAG_TILE = 512
FALLBACK_KV_TILE = 512
VMEM_LIMIT = 56 * 1024 * 1024

F32 = jnp.float32
BF16 = jnp.bfloat16


def _rms(x):
    return x * lax.rsqrt(jnp.mean(x * x, axis=-1, keepdims=True) + EPS)


def _dot(a, b):
    return jnp.dot(a, b, preferred_element_type=F32)


def _resident(shape):
    return pl.BlockSpec(shape, lambda *_: (0,) * len(shape), pipeline_mode=pl.Buffered(1))


def _resident_layer(stacked, layer):
    tail = stacked.shape[1:]
    return pl.BlockSpec((None,) + tail, lambda *_: (layer,) + (0,) * len(tail), pipeline_mode=pl.Buffered(1))


def _mlp_tail(x1, g_ref, w1_ref, w2_ref):
    h = (_rms(x1) * g_ref[...]).astype(BF16)
    acc = x1
    for c in range(D_FF // FF_CHUNK):
        cols = slice(c * FF_CHUNK, (c + 1) * FF_CHUNK)
        u = jnp.maximum(_dot(h, w1_ref[:, cols]), 0.0)
        acc = acc + _dot((u * u).astype(BF16), w2_ref[cols, :])
    return acc


def _dot_nt(a, b):
    return lax.dot_general(a, b, (((1,), (1,)), ((), ())), preferred_element_type=F32)


def _qkv_kernel(x_ref, cos_ref, sin_ref, w_down_ref, g_qa_ref, g_kva_ref, w_q_ref, w_kn_ref, w_vt_ref,
                gq_n_ref, gq_r_ref, gq_rs_ref, gk_n_ref, gk_r_ref, gk_rs_ref, q_pad_ref, q_ref, k_ref, vt_ref,
                *, q_scale):
    lane = lax.broadcasted_iota(jnp.int32, (1, LANES), 1)
    is_rope = lane < ROPE
    k_pad = jnp.where(lane == ROPE, 1.0, 0.0)
    q_off = N_HEADS * NOPE
    d_eps = QK_HEAD_DIM * EPS
    sqrt_d = math.sqrt(QK_HEAD_DIM)
    gq_n, gq_r, gq_rs = (g[...] * (sqrt_d * q_scale) for g in (gq_n_ref, gq_r_ref, gq_rs_ref))
    gk_n, gk_r, gk_rs = (g[...] * sqrt_d for g in (gk_n_ref, gk_r_ref, gk_rs_ref))

    for t in range(x_ref.shape[1] // QKV_SUB_TILE):
        rows = slice(t * QKV_SUB_TILE, (t + 1) * QKV_SUB_TILE)
        x = x_ref[0, rows, :]
        a = _dot(x.astype(BF16), w_down_ref[...]) * lax.rsqrt(jnp.mean(x * x, axis=-1, keepdims=True) + EPS)
        c_q = (_rms(a[:, :Q_LORA]) * g_qa_ref[...]).astype(BF16)
        c_kv = (_rms(a[:, Q_LORA:Q_LORA + KV_LORA]) * g_kva_ref[...]).astype(BF16)
        kpe = a[:, Q_LORA + KV_LORA:Q_LORA + KV_LORA + LANES]
        kpe_sw = a[:, Q_LORA + KV_LORA + LANES:]
        qa = _dot(c_q, w_q_ref[...])
        kna = _dot(c_kv, w_kn_ref[...])
        vta = _dot_nt(w_vt_ref[...], c_kv)

        cos2 = jnp.transpose(cos_ref[0, :, rows])
        sin2 = jnp.transpose(sin_ref[0, :, rows])
        k_rot = kpe * (gk_r * cos2) + kpe_sw * (gk_rs * sin2)
        ssq_kpe = jnp.sum(kpe * kpe, axis=-1, keepdims=True) + d_eps
        q_cos = gq_r * cos2
        q_sin = gq_rs * sin2

        for hd in range(N_HEADS):
            qn = qa[:, hd * NOPE:(hd + 1) * NOPE]
            qr = qa[:, q_off + hd * LANES:q_off + (hd + 1) * LANES]
            qr_sw = qa[:, q_off + (N_HEADS + hd) * LANES:q_off + (N_HEADS + hd + 1) * LANES]
            ssq = jnp.sum(qn * qn, axis=-1, keepdims=True) + jnp.sum(qr * qr, axis=-1, keepdims=True)
            r = lax.rsqrt(ssq + d_eps)
            q_ref[0, hd, rows, :NOPE] = (qn * r * gq_n).astype(BF16)
            q_rot = (qr * q_cos + qr_sw * q_sin) * r
            q_ref[0, hd, rows, NOPE:] = jnp.where(is_rope, q_rot, q_pad_ref[...]).astype(BF16)

            kn = kna[:, hd * NOPE:(hd + 1) * NOPE]
            r = lax.rsqrt(jnp.sum(kn * kn, axis=-1, keepdims=True) + ssq_kpe)
            k_ref[0, hd, rows, :NOPE] = (kn * r * gk_n).astype(BF16)
            k_ref[0, hd, rows, NOPE:] = jnp.where(is_rope, k_rot * r, k_pad).astype(BF16)
            vt_ref[0, hd, :, rows] = vta[hd * V_DIM:(hd + 1) * V_DIM, :].astype(BF16)


def _qkv_call(x, cos2, sin2, w):
    b, s, _ = x.shape
    tm = QKV_TILE
    row = lambda n: _resident((1, n))
    out_shape = (jax.ShapeDtypeStruct((b, N_HEADS, s, QK_PAD), BF16),
                 jax.ShapeDtypeStruct((b, N_HEADS, s, QK_PAD), BF16),
                 jax.ShapeDtypeStruct((b, N_HEADS, V_DIM, s), BF16))
    return pl.pallas_call(
        functools.partial(_qkv_kernel, q_scale=Q_SCALE),
        grid=(b, s // tm),
        in_specs=[
            pl.BlockSpec((1, tm, D_MODEL), lambda bi, i: (bi, i, 0)),
            pl.BlockSpec((1, LANES, tm), lambda bi, i: (bi, 0, i)),
            pl.BlockSpec((1, LANES, tm), lambda bi, i: (bi, 0, i)),
            _resident(w['w_down'].shape), row(Q_LORA), row(KV_LORA),
            _resident(w['w_q'].shape), _resident(w['w_kn'].shape), _resident(w['w_vt'].shape),
            row(NOPE), row(LANES), row(LANES), row(NOPE), row(LANES), row(LANES), row(LANES),
        ],
        out_specs=(
            pl.BlockSpec((1, N_HEADS, tm, QK_PAD), lambda bi, i: (bi, 0, i, 0)),
            pl.BlockSpec((1, N_HEADS, tm, QK_PAD), lambda bi, i: (bi, 0, i, 0)),
            pl.BlockSpec((1, N_HEADS, V_DIM, tm), lambda bi, i: (bi, 0, 0, i)),
        ),
        out_shape=out_shape,
        compiler_params=pltpu.CompilerParams(
            dimension_semantics=("parallel", "parallel"), vmem_limit_bytes=VMEM_LIMIT),
        name="qkv_proj",
    )(x, cos2, sin2, w['w_down'], w['g_qa'], w['g_kva'], w['w_q'], w['w_kn'], w['w_vt'],
      w['gq_n'], w['gq_r'], w['gq_rs'], w['gk_n'], w['gk_r'], w['gk_rs'], w['q_pad'])


def _flash_fixed_ref(q_ref, k_ref, vt_ref, o_ref, acc_t_ref):
    i = pl.program_id(2)
    tq, step, dg = Q_TILE, KV_STEP, DIAG_TILE
    ones_row = jnp.where(lax.broadcasted_iota(jnp.int32, (V_EXTRA_ROWS, 1), 0) == 0, 1.0, 0.0).astype(BF16)

    def weighted_values_t(p_t, off, n):
        v_t = jnp.concatenate([vt_ref[0, 0, :, pl.ds(off, n)], jnp.broadcast_to(ones_row, (V_EXTRA_ROWS, n))], axis=0)
        return _dot(v_t, p_t.astype(BF16))

    acc_t_ref[...] = jnp.zeros(acc_t_ref.shape, F32)

    def full_step(j, carry):
        off = pl.multiple_of(j * step, step)
        s_t = _dot_nt(k_ref[0, 0, pl.ds(off, step), :], q_ref[0, 0])
        acc_t_ref[...] += weighted_values_t(jnp.exp2(s_t), off, step)
        return carry

    lax.fori_loop(0, i * (tq // step), full_step, 0)

    base = pl.multiple_of(i * tq, tq)
    for r in range(tq // dg):
        cols = slice(r * dg, (r + 1) * dg)
        n = (r + 1) * dg
        s_t = _dot_nt(k_ref[0, 0, pl.ds(base, n), :], q_ref[0, 0, cols, :])
        visible = (lax.broadcasted_iota(jnp.int32, (n, dg), 0)
                   <= lax.broadcasted_iota(jnp.int32, (n, dg), 1) + r * dg)
        acc_t = acc_t_ref[:, cols] + weighted_values_t(jnp.exp2(jnp.where(visible, s_t, NEG)), base, n)
        o_ref[0, cols, :] = jnp.transpose(acc_t[:V_DIM, :] / acc_t[V_DIM:V_DIM + 1, :]).astype(BF16)


def _flash_running_max(q_ref, k_ref, vt_ref, o_ref, m_ref, l_ref, acc_ref):
    i = pl.program_id(2)
    tq, tk = Q_TILE, FALLBACK_KV_TILE
    m_ref[...] = jnp.full(m_ref.shape, NEG, F32)
    l_ref[...] = jnp.zeros(l_ref.shape, F32)
    acc_ref[...] = jnp.zeros(acc_ref.shape, F32)

    def step(off, mask):
        s = _dot_nt(q_ref[0, 0], k_ref[0, 0, pl.ds(off, tk), :])
        if mask is not None:
            s = jnp.where(mask, s, NEG)
        m_old = m_ref[...]
        m_new = jnp.maximum(m_old, jnp.max(s, axis=-1, keepdims=True))
        alpha = jnp.exp2(m_old - m_new)
        p = jnp.exp2(s - m_new)
        l_ref[...] = alpha * l_ref[...] + jnp.sum(p, axis=-1, keepdims=True)
        acc_ref[...] = alpha * acc_ref[...] + _dot_nt(p.astype(BF16), vt_ref[0, 0, :, pl.ds(off, tk)])
        m_ref[...] = m_new

    def full_step(j, carry):
        step(pl.multiple_of(j * tk, tk), None)
        return carry

    lax.fori_loop(0, i * (tq // tk), full_step, 0)
    row = lax.broadcasted_iota(jnp.int32, (tq, tk), 0)
    col = lax.broadcasted_iota(jnp.int32, (tq, tk), 1)
    for d in range(tq // tk):
        step(pl.multiple_of(i * tq + d * tk, tk), row >= col + d * tk)
    o_ref[0] = (acc_ref[...] / l_ref[...]).astype(BF16)


def _flash_kernel(fixed_ref_ok, q_ref, k_ref, vt_ref, o_ref, acc_t_ref, m_ref, l_ref, acc_ref):
    @pl.when(fixed_ref_ok[0] == 1)
    def _():
        _flash_fixed_ref(q_ref, k_ref, vt_ref, o_ref, acc_t_ref)

    @pl.when(fixed_ref_ok[0] == 0)
    def _():
        _flash_running_max(q_ref, k_ref, vt_ref, o_ref, m_ref, l_ref, acc_ref)


def _flash_call(fixed_ref_ok, q, k, v_t):
    b, nh, s, _ = q.shape
    tq = Q_TILE
    grid_spec = pltpu.PrefetchScalarGridSpec(
        num_scalar_prefetch=1,
        grid=(b, nh, s // tq),
        in_specs=[
            pl.BlockSpec((1, 1, tq, QK_PAD), lambda bi, h, i, _: (bi, h, i, 0)),
            pl.BlockSpec((1, 1, s, QK_PAD), lambda bi, h, i, _: (bi, h, 0, 0)),
            pl.BlockSpec((1, 1, V_DIM, s), lambda bi, h, i, _: (bi, h, 0, 0)),
        ],
        out_specs=pl.BlockSpec((1, tq, V_DIM), lambda bi, h, i, _: (bi, i, h)),
        scratch_shapes=[pltpu.VMEM((V_DIM + V_EXTRA_ROWS, tq), F32),
                        pltpu.VMEM((tq, 1), F32), pltpu.VMEM((tq, 1), F32), pltpu.VMEM((tq, V_DIM), F32)],
    )
    return pl.pallas_call(
        _flash_kernel,
        grid_spec=grid_spec,
        out_shape=jax.ShapeDtypeStruct((b, s, nh * V_DIM), BF16),
        compiler_params=pltpu.CompilerParams(
            dimension_semantics=("parallel", "parallel", "arbitrary"), vmem_limit_bytes=VMEM_LIMIT),
        name="flash_attn",
    )(fixed_ref_ok, q, k, v_t)


def _attn_out_mlp_kernel(x_ref, o_ref, w_o_ref, g_ref, w1_ref, w2_ref, out_ref):
    x1 = x_ref[...] + _dot(o_ref[...], w_o_ref[...])
    out_ref[...] = _mlp_tail(x1, g_ref, w1_ref, w2_ref)


def _attn_out_mlp_call(x2d, o2d, w_o, g_mlp, w1, w2, attn_layer, layer):
    t, _ = x2d.shape
    tm = TOKEN_TILE
    return pl.pallas_call(
        _attn_out_mlp_kernel,
        grid=(t // tm,),
        in_specs=[
            pl.BlockSpec((tm, D_MODEL), lambda i: (i, 0)),
            pl.BlockSpec((tm, N_HEADS * V_DIM), lambda i: (i, 0)),
            _resident_layer(w_o, attn_layer), _resident((1, D_MODEL)),
            _resident_layer(w1, layer), _resident_layer(w2, layer),
        ],
        out_specs=pl.BlockSpec((tm, D_MODEL), lambda i: (i, 0)),
        out_shape=jax.ShapeDtypeStruct((t, D_MODEL), F32),
        compiler_params=pltpu.CompilerParams(dimension_semantics=("parallel",), vmem_limit_bytes=VMEM_LIMIT),
        name="attn_out_mlp",
    )(x2d, o2d, w_o, g_mlp, w1, w2)


def _conv_mlp_kernel(x_ref, g_mix_ref, w_in_ref, cw_ref, w_out_ref, g_ref, w1_ref, w2_ref, out_ref, u_buf):
    tm = x_ref.shape[1]

    @pl.when(pl.program_id(1) == 0)
    def _():
        u_buf[:HALO, :] = jnp.zeros((HALO, D_MODEL), F32)

    x = x_ref[0]
    h = (_rms(x) * g_mix_ref[...]).astype(BF16)
    gate_c = _dot(h, w_in_ref[:, D_MODEL:2 * D_MODEL])
    u = gate_c * _dot(h, w_in_ref[:, 2 * D_MODEL:])
    u_buf[HALO:, :] = u
    conv = (u_buf[HALO - 2:HALO - 2 + tm, :] * cw_ref[0:1, :]
            + u_buf[HALO - 1:HALO - 1 + tm, :] * cw_ref[1:2, :]
            + u * cw_ref[2:3, :])
    u_buf[:HALO, :] = u_buf[tm:tm + HALO, :]
    gate_b = _dot(h, w_in_ref[:, :D_MODEL])
    x1 = x + _dot((gate_b * conv).astype(BF16), w_out_ref[...])
    out_ref[0] = _mlp_tail(x1, g_ref, w1_ref, w2_ref)


def _conv_mlp_call(x, g_mix, w_in, conv_w, w_out, g_mlp, w1, w2, conv_layer, layer):
    b, s, _ = x.shape
    tm = TOKEN_TILE
    return pl.pallas_call(
        _conv_mlp_kernel,
        grid=(b, s // tm),
        in_specs=[
            pl.BlockSpec((1, tm, D_MODEL), lambda bi, i: (bi, i, 0)),
            _resident((1, D_MODEL)), _resident_layer(w_in, conv_layer), _resident_layer(conv_w, conv_layer),
            _resident_layer(w_out, conv_layer),
            _resident((1, D_MODEL)), _resident_layer(w1, layer), _resident_layer(w2, layer),
        ],
        out_specs=pl.BlockSpec((1, tm, D_MODEL), lambda bi, i: (bi, i, 0)),
        out_shape=jax.ShapeDtypeStruct((b, s, D_MODEL), F32),
        scratch_shapes=[pltpu.VMEM((tm + HALO, D_MODEL), F32)],
        compiler_params=pltpu.CompilerParams(
            dimension_semantics=("parallel", "arbitrary"), vmem_limit_bytes=VMEM_LIMIT),
        name="conv_mlp",
    )(x, g_mix, w_in, conv_w, w_out, g_mlp, w1, w2)


def _swap_halves(w):
    half = w.shape[-1] // 2
    return jnp.concatenate([w[..., half:], w[..., :half]], axis=-1)


def _attn_params(g_mix, w_down, g_q_a, g_kv_a, w_uq, w_ukv, g_qnorm, g_knorm):
    pad = lambda w: jnp.pad(w, [(0, 0)] * (w.ndim - 1) + [(0, LANES - w.shape[-1])])
    kpe = w_down[:, Q_LORA + KV_LORA:]
    w_down_ext = jnp.concatenate([w_down[:, :Q_LORA + KV_LORA], pad(kpe), pad(_swap_halves(kpe))], axis=1)
    w_down_ext = g_mix[:, None] * w_down_ext
    uq = w_uq.reshape(Q_LORA, N_HEADS, QK_HEAD_DIM)
    uq_r = uq[:, :, NOPE:]
    w_q = jnp.concatenate([uq[:, :, :NOPE].reshape(Q_LORA, -1), pad(uq_r).reshape(Q_LORA, -1),
                           pad(_swap_halves(uq_r)).reshape(Q_LORA, -1)], axis=1)
    ukv = w_ukv.reshape(KV_LORA, N_HEADS, NOPE + V_DIM)
    w_kn = ukv[:, :, :NOPE].reshape(KV_LORA, -1)
    w_vt = ukv[:, :, NOPE:].reshape(KV_LORA, -1).T
    row = lambda g: pad(g)[None, :]
    bound = (BOUND_MARGIN * Q_SCALE * QK_HEAD_DIM) * jnp.max(jnp.abs(g_qnorm)) * jnp.max(jnp.abs(g_knorm))
    fixed_ref_ok = (2.0 * bound < MAX_FIXED_REF_SPAN).astype(jnp.int32).reshape(1)
    q_pad = jnp.where(jnp.arange(LANES) == ROPE, -bound, 0.0)[None, :]
    return {
        'w_down': w_down_ext.astype(BF16), 'g_qa': g_q_a[None, :], 'g_kva': g_kv_a[None, :],
        'w_q': w_q.astype(BF16), 'w_kn': w_kn.astype(BF16), 'w_vt': w_vt.astype(BF16),
        'gq_n': g_qnorm[None, :NOPE], 'gq_r': row(g_qnorm[NOPE:]), 'gq_rs': row(_swap_halves(g_qnorm[NOPE:])),
        'gk_n': g_knorm[None, :NOPE], 'gk_r': row(g_knorm[NOPE:]), 'gk_rs': row(_swap_halves(g_knorm[NOPE:])),
        'q_pad': q_pad, 'fixed_ref_ok': fixed_ref_ok,
    }


def _rope_tables(positions):
    inv_freq = ROPE_THETA ** (-jnp.arange(0, ROPE, 2, dtype=F32) / ROPE)
    ang = positions.astype(F32)[:, None, :] * inv_freq[None, :, None]
    cos, sin = jnp.cos(ang), jnp.sin(ang)
    zeros = jnp.zeros((positions.shape[0], LANES - ROPE, positions.shape[1]), F32)
    return jnp.concatenate([cos, cos, zeros], axis=1), jnp.concatenate([-sin, sin, zeros], axis=1)


def kernel(x, positions, g_mix, g_mlp, attn_w_down, attn_g_q_a, attn_g_kv_a, attn_w_uq, attn_w_ukv, attn_g_qnorm,
           attn_g_knorm, attn_w_o, conv_w_in, conv_w, conv_w_out, mlp_w1, mlp_w2):
    b, s, d = x.shape
    depth = g_mix.shape[0]
    cos2, sin2 = _rope_tables(positions)
    w1, w2 = mlp_w1.astype(BF16), mlp_w2.astype(BF16)
    w_o, w_in, w_out = attn_w_o.astype(BF16), conv_w_in.astype(BF16), conv_w_out.astype(BF16)
    for i in range(depth):
        j = i // 2
        if i % 2 == 0:
            w = _attn_params(g_mix[i], attn_w_down[j], attn_g_q_a[j], attn_g_kv_a[j], attn_w_uq[j], attn_w_ukv[j],
                             attn_g_qnorm[j], attn_g_knorm[j])
            q, k, v_t = _qkv_call(x, cos2, sin2, w)
            o = _flash_call(w['fixed_ref_ok'], q, k, v_t)
            x = _attn_out_mlp_call(x.reshape(b * s, d), o.reshape(b * s, d), w_o, g_mlp[i][None, :], w1, w2,
                                   j, i).reshape(b, s, d)
        else:
            x = _conv_mlp_call(x, g_mix[i][None, :], w_in, conv_w, w_out, g_mlp[i][None, :], w1, w2, j, i)
    return x
```

```python
import functools
import math

import jax
import jax.numpy as jnp
from jax import lax
from jax.experimental import pallas as pl
from jax.experimental.pallas import tpu as pltpu

D_MODEL = 1024
N_HEADS = 8
NOPE = 128
ROPE = 64
QK_HEAD_DIM = NOPE + ROPE
V_DIM = 128
Q_LORA = 256
KV_LORA = 128
D_FF = 4096
CONV_WIDTH = 3
EPS = 1e-6
ROPE_THETA = 10000.0

LANES = 128
QK_PAD = 2 * LANES
V_EXTRA_ROWS = 16
HALO = 8
NEG = -1e30

Q_SCALE = QK_HEAD_DIM ** -0.5 * math.log2(math.e)
MAX_FIXED_REF_SPAN = 100.0
BOUND_MARGIN = 1.02

TOKEN_TILE = 512
QKV_TILE = 1024
QKV_SUB_TILE = 512
FF_CHUNK = 1024
Q_TILE = 2048
KV_STEP = 2048
DIAG_TILE = 512
FALLBACK_KV_TILE = 512
VMEM_LIMIT = 56 * 1024 * 1024

F32 = jnp.float32
BF16 = jnp.bfloat16


def _rms(x):
    return x * lax.rsqrt(jnp.mean(x * x, axis=-1, keepdims=True) + EPS)


def _dot(a, b):
    return jnp.dot(a, b, preferred_element_type=F32)


def _resident(shape):
    return pl.BlockSpec(shape, lambda *_: (0,) * len(shape), pipeline_mode=pl.Buffered(1))


def _resident_layer(stacked, layer):
    tail = stacked.shape[1:]
    return pl.BlockSpec((None,) + tail, lambda *_: (layer,) + (0,) * len(tail), pipeline_mode=pl.Buffered(1))


def _mlp_tail(x1, g_ref, w1_ref, w2_ref):
    h = (_rms(x1) * g_ref[...]).astype(BF16)
    acc = x1
    for c in range(D_FF // FF_CHUNK):
        cols = slice(c * FF_CHUNK, (c + 1) * FF_CHUNK)
        u = jnp.maximum(_dot(h, w1_ref[:, cols]), 0.0)
        acc = acc + _dot((u * u).astype(BF16), w2_ref[cols, :])
    return acc


def _dot_nt(a, b):
    return lax.dot_general(a, b, (((1,), (1,)), ((), ())), preferred_element_type=F32)


def _qkv_kernel(x_ref, cos_ref, sin_ref, w_down_ref, g_qa_ref, g_kva_ref, w_q_ref, w_kn_ref, w_vt_ref,
                gq_n_ref, gq_r_ref, gq_rs_ref, gk_n_ref, gk_r_ref, gk_rs_ref, q_pad_ref, q_ref, k_ref, vt_ref,
                *, q_scale):
    lane = lax.broadcasted_iota(jnp.int32, (1, LANES), 1)
    is_rope = lane < ROPE
    k_pad = jnp.where(lane == ROPE, 1.0, 0.0)
    q_off = N_HEADS * NOPE
    d_eps = QK_HEAD_DIM * EPS
    sqrt_d = math.sqrt(QK_HEAD_DIM)
    gq_n, gq_r, gq_rs = (g[...] * (sqrt_d * q_scale) for g in (gq_n_ref, gq_r_ref, gq_rs_ref))
    gk_n, gk_r, gk_rs = (g[...] * sqrt_d for g in (gk_n_ref, gk_r_ref, gk_rs_ref))

    for t in range(x_ref.shape[1] // QKV_SUB_TILE):
        rows = slice(t * QKV_SUB_TILE, (t + 1) * QKV_SUB_TILE)
        x = x_ref[0, rows, :]
        a = _dot(x.astype(BF16), w_down_ref[...]) * lax.rsqrt(jnp.mean(x * x, axis=-1, keepdims=True) + EPS)
        c_q = (_rms(a[:, :Q_LORA]) * g_qa_ref[...]).astype(BF16)
        c_kv = (_rms(a[:, Q_LORA:Q_LORA + KV_LORA]) * g_kva_ref[...]).astype(BF16)
        kpe = a[:, Q_LORA + KV_LORA:Q_LORA + KV_LORA + LANES]
        kpe_sw = a[:, Q_LORA + KV_LORA + LANES:]
        qa = _dot(c_q, w_q_ref[...])
        kna = _dot(c_kv, w_kn_ref[...])
        vta = _dot_nt(w_vt_ref[...], c_kv)

        cos2 = jnp.transpose(cos_ref[0, :, rows])
        sin2 = jnp.transpose(sin_ref[0, :, rows])
        k_rot = kpe * (gk_r * cos2) + kpe_sw * (gk_rs * sin2)
        ssq_kpe = jnp.sum(kpe * kpe, axis=-1, keepdims=True) + d_eps
        q_cos = gq_r * cos2
        q_sin = gq_rs * sin2

        for hd in range(N_HEADS):
            qn = qa[:, hd * NOPE:(hd + 1) * NOPE]
            qr = qa[:, q_off + hd * LANES:q_off + (hd + 1) * LANES]
            qr_sw = qa[:, q_off + (N_HEADS + hd) * LANES:q_off + (N_HEADS + hd + 1) * LANES]
            ssq = jnp.sum(qn * qn, axis=-1, keepdims=True) + jnp.sum(qr * qr, axis=-1, keepdims=True)
            r = lax.rsqrt(ssq + d_eps)
            q_ref[0, hd, rows, :NOPE] = (qn * r * gq_n).astype(BF16)
            q_rot = (qr * q_cos + qr_sw * q_sin) * r
            q_ref[0, hd, rows, NOPE:] = jnp.where(is_rope, q_rot, q_pad_ref[...]).astype(BF16)

            kn = kna[:, hd * NOPE:(hd + 1) * NOPE]
            r = lax.rsqrt(jnp.sum(kn * kn, axis=-1, keepdims=True) + ssq_kpe)
            k_ref[0, hd, rows, :NOPE] = (kn * r * gk_n).astype(BF16)
            k_ref[0, hd, rows, NOPE:] = jnp.where(is_rope, k_rot * r, k_pad).astype(BF16)
            vt_ref[0, hd, :, rows] = vta[hd * V_DIM:(hd + 1) * V_DIM, :].astype(BF16)


def _qkv_call(x, cos2, sin2, w):
    b, s, _ = x.shape
    tm = QKV_TILE
    row = lambda n: _resident((1, n))
    out_shape = (jax.ShapeDtypeStruct((b, N_HEADS, s, QK_PAD), BF16),
                 jax.ShapeDtypeStruct((b, N_HEADS, s, QK_PAD), BF16),
                 jax.ShapeDtypeStruct((b, N_HEADS, V_DIM, s), BF16))
    return pl.pallas_call(
        functools.partial(_qkv_kernel, q_scale=Q_SCALE),
        grid=(b, s // tm),
        in_specs=[
            pl.BlockSpec((1, tm, D_MODEL), lambda bi, i: (bi, i, 0)),
            pl.BlockSpec((1, LANES, tm), lambda bi, i: (bi, 0, i)),
            pl.BlockSpec((1, LANES, tm), lambda bi, i: (bi, 0, i)),
            _resident(w['w_down'].shape), row(Q_LORA), row(KV_LORA),
            _resident(w['w_q'].shape), _resident(w['w_kn'].shape), _resident(w['w_vt'].shape),
            row(NOPE), row(LANES), row(LANES), row(NOPE), row(LANES), row(LANES), row(LANES),
        ],
        out_specs=(
            pl.BlockSpec((1, N_HEADS, tm, QK_PAD), lambda bi, i: (bi, 0, i, 0)),
            pl.BlockSpec((1, N_HEADS, tm, QK_PAD), lambda bi, i: (bi, 0, i, 0)),
            pl.BlockSpec((1, N_HEADS, V_DIM, tm), lambda bi, i: (bi, 0, 0, i)),
        ),
        out_shape=out_shape,
        compiler_params=pltpu.CompilerParams(
            dimension_semantics=("parallel", "parallel"), vmem_limit_bytes=VMEM_LIMIT),
        name="qkv_proj",
    )(x, cos2, sin2, w['w_down'], w['g_qa'], w['g_kva'], w['w_q'], w['w_kn'], w['w_vt'],
      w['gq_n'], w['gq_r'], w['gq_rs'], w['gk_n'], w['gk_r'], w['gk_rs'], w['q_pad'])


def _flash_fixed_ref(q_ref, k_ref, vt_ref, o_ref, acc_t_ref):
    i = pl.program_id(2)
    tq, step, dg = Q_TILE, KV_STEP, DIAG_TILE
    ones_row = jnp.where(lax.broadcasted_iota(jnp.int32, (V_EXTRA_ROWS, 1), 0) == 0, 1.0, 0.0).astype(BF16)

    def weighted_values_t(p_t, off, n):
        v_t = jnp.concatenate([vt_ref[0, 0, :, pl.ds(off, n)], jnp.broadcast_to(ones_row, (V_EXTRA_ROWS, n))], axis=0)
        return _dot(v_t, p_t.astype(BF16))

    acc_t_ref[...] = jnp.zeros(acc_t_ref.shape, F32)

    def full_step(j, carry):
        off = pl.multiple_of(j * step, step)
        s_t = _dot_nt(k_ref[0, 0, pl.ds(off, step), :], q_ref[0, 0])
        acc_t_ref[...] += weighted_values_t(jnp.exp2(s_t), off, step)
        return carry

    lax.fori_loop(0, i * (tq // step), full_step, 0)

    base = pl.multiple_of(i * tq, tq)
    for r in range(tq // dg):
        cols = slice(r * dg, (r + 1) * dg)
        n = (r + 1) * dg
        s_t = _dot_nt(k_ref[0, 0, pl.ds(base, n), :], q_ref[0, 0, cols, :])
        visible = (lax.broadcasted_iota(jnp.int32, (n, dg), 0)
                   <= lax.broadcasted_iota(jnp.int32, (n, dg), 1) + r * dg)
        acc_t = acc_t_ref[:, cols] + weighted_values_t(jnp.exp2(jnp.where(visible, s_t, NEG)), base, n)
        o_ref[0, cols, :] = jnp.transpose(acc_t[:V_DIM, :] / acc_t[V_DIM:V_DIM + 1, :]).astype(BF16)


def _flash_running_max(q_ref, k_ref, vt_ref, o_ref, m_ref, l_ref, acc_ref):
    i = pl.program_id(2)
    tq, tk = Q_TILE, FALLBACK_KV_TILE
    m_ref[...] = jnp.full(m_ref.shape, NEG, F32)
    l_ref[...] = jnp.zeros(l_ref.shape, F32)
    acc_ref[...] = jnp.zeros(acc_ref.shape, F32)

    def step(off, mask):
        s = _dot_nt(q_ref[0, 0], k_ref[0, 0, pl.ds(off, tk), :])
        if mask is not None:
            s = jnp.where(mask, s, NEG)
        m_old = m_ref[...]
        m_new = jnp.maximum(m_old, jnp.max(s, axis=-1, keepdims=True))
        alpha = jnp.exp2(m_old - m_new)
        p = jnp.exp2(s - m_new)
        l_ref[...] = alpha * l_ref[...] + jnp.sum(p, axis=-1, keepdims=True)
        acc_ref[...] = alpha * acc_ref[...] + _dot_nt(p.astype(BF16), vt_ref[0, 0, :, pl.ds(off, tk)])
        m_ref[...] = m_new

    def full_step(j, carry):
        step(pl.multiple_of(j * tk, tk), None)
        return carry

    lax.fori_loop(0, i * (tq // tk), full_step, 0)
    row = lax.broadcasted_iota(jnp.int32, (tq, tk), 0)
    col = lax.broadcasted_iota(jnp.int32, (tq, tk), 1)
    for d in range(tq // tk):
        step(pl.multiple_of(i * tq + d * tk, tk), row >= col + d * tk)
    o_ref[0] = (acc_ref[...] / l_ref[...]).astype(BF16)


def _flash_kernel(fixed_ref_ok, q_ref, k_ref, vt_ref, o_ref, acc_t_ref, m_ref, l_ref, acc_ref):
    @pl.when(fixed_ref_ok[0] == 1)
    def _():
        _flash_fixed_ref(q_ref, k_ref, vt_ref, o_ref, acc_t_ref)

    @pl.when(fixed_ref_ok[0] == 0)
    def _():
        _flash_running_max(q_ref, k_ref, vt_ref, o_ref, m_ref, l_ref, acc_ref)


def _flash_call(fixed_ref_ok, q, k, v_t):
    b, nh, s, _ = q.shape
    tq = Q_TILE
    grid_spec = pltpu.PrefetchScalarGridSpec(
        num_scalar_prefetch=1,
        grid=(b, nh, s // tq),
        in_specs=[
            pl.BlockSpec((1, 1, tq, QK_PAD), lambda bi, h, i, _: (bi, h, i, 0)),
            pl.BlockSpec((1, 1, s, QK_PAD), lambda bi, h, i, _: (bi, h, 0, 0)),
            pl.BlockSpec((1, 1, V_DIM, s), lambda bi, h, i, _: (bi, h, 0, 0)),
        ],
        out_specs=pl.BlockSpec((1, tq, V_DIM), lambda bi, h, i, _: (bi, i, h)),
        scratch_shapes=[pltpu.VMEM((V_DIM + V_EXTRA_ROWS, tq), F32),
                        pltpu.VMEM((tq, 1), F32), pltpu.VMEM((tq, 1), F32), pltpu.VMEM((tq, V_DIM), F32)],
    )
    return pl.pallas_call(
        _flash_kernel,
        grid_spec=grid_spec,
        out_shape=jax.ShapeDtypeStruct((b, s, nh * V_DIM), BF16),
        compiler_params=pltpu.CompilerParams(
            dimension_semantics=("parallel", "parallel", "arbitrary"), vmem_limit_bytes=VMEM_LIMIT),
        name="flash_attn",
    )(fixed_ref_ok, q, k, v_t)


def _attn_out_mlp_kernel(x_ref, o_ref, w_o_ref, g_ref, w1_ref, w2_ref, out_ref):
    x1 = x_ref[...] + _dot(o_ref[...], w_o_ref[...])
    out_ref[...] = _mlp_tail(x1, g_ref, w1_ref, w2_ref)


def _attn_out_mlp_call(x2d, o2d, w_o, g_mlp, w1, w2, attn_layer, layer):
    t, _ = x2d.shape
    tm = TOKEN_TILE
    return pl.pallas_call(
        _attn_out_mlp_kernel,
        grid=(t // tm,),
        in_specs=[
            pl.BlockSpec((tm, D_MODEL), lambda i: (i, 0)),
            pl.BlockSpec((tm, N_HEADS * V_DIM), lambda i: (i, 0)),
            _resident_layer(w_o, attn_layer), _resident((1, D_MODEL)),
            _resident_layer(w1, layer), _resident_layer(w2, layer),
        ],
        out_specs=pl.BlockSpec((tm, D_MODEL), lambda i: (i, 0)),
        out_shape=jax.ShapeDtypeStruct((t, D_MODEL), F32),
        compiler_params=pltpu.CompilerParams(dimension_semantics=("parallel",), vmem_limit_bytes=VMEM_LIMIT),
        name="attn_out_mlp",
    )(x2d, o2d, w_o, g_mlp, w1, w2)


def _conv_mlp_kernel(x_ref, g_mix_ref, w_in_ref, cw_ref, w_out_ref, g_ref, w1_ref, w2_ref, out_ref, u_buf):
    tm = x_ref.shape[1]

    @pl.when(pl.program_id(1) == 0)
    def _():
        u_buf[:HALO, :] = jnp.zeros((HALO, D_MODEL), F32)

    x = x_ref[0]
    h = (_rms(x) * g_mix_ref[...]).astype(BF16)
    gate_c = _dot(h, w_in_ref[:, D_MODEL:2 * D_MODEL])
    u = gate_c * _dot(h, w_in_ref[:, 2 * D_MODEL:])
    u_buf[HALO:, :] = u
    conv = (u_buf[HALO - 2:HALO - 2 + tm, :] * cw_ref[0:1, :]
            + u_buf[HALO - 1:HALO - 1 + tm, :] * cw_ref[1:2, :]
            + u * cw_ref[2:3, :])
    u_buf[:HALO, :] = u_buf[tm:tm + HALO, :]
    gate_b = _dot(h, w_in_ref[:, :D_MODEL])
    x1 = x + _dot((gate_b * conv).astype(BF16), w_out_ref[...])
    out_ref[0] = _mlp_tail(x1, g_ref, w1_ref, w2_ref)


def _conv_mlp_call(x, g_mix, w_in, conv_w, w_out, g_mlp, w1, w2, conv_layer, layer):
    b, s, _ = x.shape
    tm = TOKEN_TILE
    return pl.pallas_call(
        _conv_mlp_kernel,
        grid=(b, s // tm),
        in_specs=[
            pl.BlockSpec((1, tm, D_MODEL), lambda bi, i: (bi, i, 0)),
            _resident((1, D_MODEL)), _resident_layer(w_in, conv_layer), _resident_layer(conv_w, conv_layer),
            _resident_layer(w_out, conv_layer),
            _resident((1, D_MODEL)), _resident_layer(w1, layer), _resident_layer(w2, layer),
        ],
        out_specs=pl.BlockSpec((1, tm, D_MODEL), lambda bi, i: (bi, i, 0)),
        out_shape=jax.ShapeDtypeStruct((b, s, D_MODEL), F32),
        scratch_shapes=[pltpu.VMEM((tm + HALO, D_MODEL), F32)],
        compiler_params=pltpu.CompilerParams(
            dimension_semantics=("parallel", "arbitrary"), vmem_limit_bytes=VMEM_LIMIT),
        name="conv_mlp",
    )(x, g_mix, w_in, conv_w, w_out, g_mlp, w1, w2)


def _swap_halves(w):
    half = w.shape[-1] // 2
    return jnp.concatenate([w[..., half:], w[..., :half]], axis=-1)


def _attn_params(g_mix, w_down, g_q_a, g_kv_a, w_uq, w_ukv, g_qnorm, g_knorm):
    pad = lambda w: jnp.pad(w, [(0, 0)] * (w.ndim - 1) + [(0, LANES - w.shape[-1])])
    kpe = w_down[:, Q_LORA + KV_LORA:]
    w_down_ext = jnp.concatenate([w_down[:, :Q_LORA + KV_LORA], pad(kpe), pad(_swap_halves(kpe))], axis=1)
    w_down_ext = g_mix[:, None] * w_down_ext
    uq = w_uq.reshape(Q_LORA, N_HEADS, QK_HEAD_DIM)
    uq_r = uq[:, :, NOPE:]
    w_q = jnp.concatenate([uq[:, :, :NOPE].reshape(Q_LORA, -1), pad(uq_r).reshape(Q_LORA, -1),
                           pad(_swap_halves(uq_r)).reshape(Q_LORA, -1)], axis=1)
    ukv = w_ukv.reshape(KV_LORA, N_HEADS, NOPE + V_DIM)
    w_kn = ukv[:, :, :NOPE].reshape(KV_LORA, -1)
    w_vt = ukv[:, :, NOPE:].reshape(KV_LORA, -1).T
    row = lambda g: pad(g)[None, :]
    bound = (BOUND_MARGIN * Q_SCALE * QK_HEAD_DIM) * jnp.max(jnp.abs(g_qnorm)) * jnp.max(jnp.abs(g_knorm))
    fixed_ref_ok = (2.0 * bound < MAX_FIXED_REF_SPAN).astype(jnp.int32).reshape(1)
    q_pad = jnp.where(jnp.arange(LANES) == ROPE, -bound, 0.0)[None, :]
    return {
        'w_down': w_down_ext.astype(BF16), 'g_qa': g_q_a[None, :], 'g_kva': g_kv_a[None, :],
        'w_q': w_q.astype(BF16), 'w_kn': w_kn.astype(BF16), 'w_vt': w_vt.astype(BF16),
        'gq_n': g_qnorm[None, :NOPE], 'gq_r': row(g_qnorm[NOPE:]), 'gq_rs': row(_swap_halves(g_qnorm[NOPE:])),
        'gk_n': g_knorm[None, :NOPE], 'gk_r': row(g_knorm[NOPE:]), 'gk_rs': row(_swap_halves(g_knorm[NOPE:])),
        'q_pad': q_pad, 'fixed_ref_ok': fixed_ref_ok,
    }


def _rope_tables(positions):
    inv_freq = ROPE_THETA ** (-jnp.arange(0, ROPE, 2, dtype=F32) / ROPE)
    ang = positions.astype(F32)[:, None, :] * inv_freq[None, :, None]
    cos, sin = jnp.cos(ang), jnp.sin(ang)
    zeros = jnp.zeros((positions.shape[0], LANES - ROPE, positions.shape[1]), F32)
    return jnp.concatenate([cos, cos, zeros], axis=1), jnp.concatenate([-sin, sin, zeros], axis=1)


def kernel(x, positions, g_mix, g_mlp, attn_w_down, attn_g_q_a, attn_g_kv_a, attn_w_uq, attn_w_ukv, attn_g_qnorm,
           attn_g_knorm, attn_w_o, conv_w_in, conv_w, conv_w_out, mlp_w1, mlp_w2):
    b, s, d = x.shape
    depth = g_mix.shape[0]
    cos2, sin2 = _rope_tables(positions)
    w1, w2 = mlp_w1.astype(BF16), mlp_w2.astype(BF16)
    w_o, w_in, w_out = attn_w_o.astype(BF16), conv_w_in.astype(BF16), conv_w_out.astype(BF16)
    for i in range(depth):
        j = i // 2
        if i % 2 == 0:
            w = _attn_params(g_mix[i], attn_w_down[j], attn_g_q_a[j], attn_g_kv_a[j], attn_w_uq[j], attn_w_ukv[j],
                             attn_g_qnorm[j], attn_g_knorm[j])
            q, k, v_t = _qkv_call(x, cos2, sin2, w)
            o = _flash_call(w['fixed_ref_ok'], q, k, v_t)
            x = _attn_out_mlp_call(x.reshape(b * s, d), o.reshape(b * s, d), w_o, g_mlp[i][None, :], w1, w2,
                                   j, i).reshape(b, s, d)
        else:
            x = _conv_mlp_call(x, g_mix[i][None, :], w_in, conv_w, w_out, g_mlp[i][None, :], w1, w2, j, i)
    return x
```

```python
import functools
import math

import jax
import jax.numpy as jnp
from jax import lax
from jax.experimental import pallas as pl
from jax.experimental.pallas import tpu as pltpu

D_MODEL = 1024
N_HEADS = 8
NOPE = 128
ROPE = 64
QK_HEAD_DIM = NOPE + ROPE
V_DIM = 128
Q_LORA = 256
KV_LORA = 128
D_FF = 4096
CONV_WIDTH = 3
EPS = 1e-6
ROPE_THETA = 10000.0

LANES = 128
QK_PAD = 2 * LANES
V_EXTRA_ROWS = 16
HALO = 8
NEG = -1e30

Q_SCALE = QK_HEAD_DIM ** -0.5 * math.log2(math.e)
MAX_FIXED_REF_SPAN = 100.0
BOUND_MARGIN = 1.02

TOKEN_TILE = 512
QKV_TILE = 1024
QKV_SUB_TILE = 512
FF_CHUNK = 1024
Q_TILE = 2048
KV_STEP = 2048
DIAG_TILE = 512
FALLBACK_KV_TILE = 512
VMEM_LIMIT = 56 * 1024 * 1024

F32 = jnp.float32
BF16 = jnp.bfloat16


def _rms(x):
    return x * lax.rsqrt(jnp.mean(x * x, axis=-1, keepdims=True) + EPS)


def _dot(a, b):
    return jnp.dot(a, b, preferred_element_type=F32)


def _resident(shape):
    return pl.BlockSpec(shape, lambda *_: (0,) * len(shape), pipeline_mode=pl.Buffered(1))


def _resident_layer(stacked, layer):
    tail = stacked.shape[1:]
    return pl.BlockSpec((None,) + tail, lambda *_: (layer,) + (0,) * len(tail), pipeline_mode=pl.Buffered(1))


def _mlp_tail(x1, g_ref, w1_ref, w2_ref):
    h = (_rms(x1) * g_ref[...]).astype(BF16)
    acc = x1
    for c in range(D_FF // FF_CHUNK):
        cols = slice(c * FF_CHUNK, (c + 1) * FF_CHUNK)
        u = jnp.maximum(_dot(h, w1_ref[:, cols]), 0.0)
        acc = acc + _dot((u * u).astype(BF16), w2_ref[cols, :])
    return acc


def _dot_nt(a, b):
    return lax.dot_general(a, b, (((1,), (1,)), ((), ())), preferred_element_type=F32)


def _qkv_kernel(x_ref, cos_ref, sin_ref, w_down_ref, g_qa_ref, g_kva_ref, w_q_ref, w_kn_ref, w_vt_ref,
                gq_n_ref, gq_r_ref, gq_rs_ref, gk_n_ref, gk_r_ref, gk_rs_ref, q_pad_ref, q_ref, k_ref, vt_ref,
                *, q_scale):
    lane = lax.broadcasted_iota(jnp.int32, (1, LANES), 1)
    is_rope = lane < ROPE
    k_pad = jnp.where(lane == ROPE, 1.0, 0.0)
    q_off = N_HEADS * NOPE
    d_eps = QK_HEAD_DIM * EPS
    sqrt_d = math.sqrt(QK_HEAD_DIM)
    gq_n, gq_r, gq_rs = (g[...] * (sqrt_d * q_scale) for g in (gq_n_ref, gq_r_ref, gq_rs_ref))
    gk_n, gk_r, gk_rs = (g[...] * sqrt_d for g in (gk_n_ref, gk_r_ref, gk_rs_ref))

    for t in range(x_ref.shape[1] // QKV_SUB_TILE):
        rows = slice(t * QKV_SUB_TILE, (t + 1) * QKV_SUB_TILE)
        x = x_ref[0, rows, :]
        a = _dot(x.astype(BF16), w_down_ref[...]) * lax.rsqrt(jnp.mean(x * x, axis=-1, keepdims=True) + EPS)
        c_q = (_rms(a[:, :Q_LORA]) * g_qa_ref[...]).astype(BF16)
        c_kv = (_rms(a[:, Q_LORA:Q_LORA + KV_LORA]) * g_kva_ref[...]).astype(BF16)
        kpe = a[:, Q_LORA + KV_LORA:Q_LORA + KV_LORA + LANES]
        kpe_sw = a[:, Q_LORA + KV_LORA + LANES:]
        qa = _dot(c_q, w_q_ref[...])
        kna = _dot(c_kv, w_kn_ref[...])
        vta = _dot_nt(w_vt_ref[...], c_kv)

        cos2 = jnp.transpose(cos_ref[0, :, rows])
        sin2 = jnp.transpose(sin_ref[0, :, rows])
        k_rot = kpe * (gk_r * cos2) + kpe_sw * (gk_rs * sin2)
        ssq_kpe = jnp.sum(kpe * kpe, axis=-1, keepdims=True) + d_eps
        q_cos = gq_r * cos2
        q_sin = gq_rs * sin2

        for hd in range(N_HEADS):
            qn = qa[:, hd * NOPE:(hd + 1) * NOPE]
            qr = qa[:, q_off + hd * LANES:q_off + (hd + 1) * LANES]
            qr_sw = qa[:, q_off + (N_HEADS + hd) * LANES:q_off + (N_HEADS + hd + 1) * LANES]
            ssq = jnp.sum(qn * qn, axis=-1, keepdims=True) + jnp.sum(qr * qr, axis=-1, keepdims=True)
            r = lax.rsqrt(ssq + d_eps)
            q_ref[0, hd, rows, :NOPE] = (qn * r * gq_n).astype(BF16)
            q_rot = (qr * q_cos + qr_sw * q_sin) * r
            q_ref[0, hd, rows, NOPE:] = jnp.where(is_rope, q_rot, q_pad_ref[...]).astype(BF16)

            kn = kna[:, hd * NOPE:(hd + 1) * NOPE]
            r = lax.rsqrt(jnp.sum(kn * kn, axis=-1, keepdims=True) + ssq_kpe)
            k_ref[0, hd, rows, :NOPE] = (kn * r * gk_n).astype(BF16)
            k_ref[0, hd, rows, NOPE:] = jnp.where(is_rope, k_rot * r, k_pad).astype(BF16)
            vt_ref[0, hd, :, rows] = vta[hd * V_DIM:(hd + 1) * V_DIM, :].astype(BF16)


def _qkv_call(x, cos2, sin2, w):
    b, s, _ = x.shape
    tm = QKV_TILE
    row = lambda n: _resident((1, n))
    out_shape = (jax.ShapeDtypeStruct((b, N_HEADS, s, QK_PAD), BF16),
                 jax.ShapeDtypeStruct((b, N_HEADS, s, QK_PAD), BF16),
                 jax.ShapeDtypeStruct((b, N_HEADS, V_DIM, s), BF16))
    return pl.pallas_call(
        functools.partial(_qkv_kernel, q_scale=Q_SCALE),
        grid=(b, s // tm),
        in_specs=[
            pl.BlockSpec((1, tm, D_MODEL), lambda bi, i: (bi, i, 0)),
            pl.BlockSpec((1, LANES, tm), lambda bi, i: (bi, 0, i)),
            pl.BlockSpec((1, LANES, tm), lambda bi, i: (bi, 0, i)),
            _resident(w['w_down'].shape), row(Q_LORA), row(KV_LORA),
            _resident(w['w_q'].shape), _resident(w['w_kn'].shape), _resident(w['w_vt'].shape),
            row(NOPE), row(LANES), row(LANES), row(NOPE), row(LANES), row(LANES), row(LANES),
        ],
        out_specs=(
            pl.BlockSpec((1, N_HEADS, tm, QK_PAD), lambda bi, i: (bi, 0, i, 0)),
            pl.BlockSpec((1, N_HEADS, tm, QK_PAD), lambda bi, i: (bi, 0, i, 0)),
            pl.BlockSpec((1, N_HEADS, V_DIM, tm), lambda bi, i: (bi, 0, 0, i)),
        ),
        out_shape=out_shape,
        compiler_params=pltpu.CompilerParams(
            dimension_semantics=("parallel", "parallel"), vmem_limit_bytes=VMEM_LIMIT),
        name="qkv_proj",
    )(x, cos2, sin2, w['w_down'], w['g_qa'], w['g_kva'], w['w_q'], w['w_kn'], w['w_vt'],
      w['gq_n'], w['gq_r'], w['gq_rs'], w['gk_n'], w['gk_r'], w['gk_rs'], w['q_pad'])


def _flash_fixed_ref(q_ref, k_ref, vt_ref, o_ref, acc_t_ref):
    tq, step, dg = Q_TILE, KV_STEP, DIAG_TILE
    ones_row = jnp.where(lax.broadcasted_iota(jnp.int32, (V_EXTRA_ROWS, 1), 0) == 0, 1.0, 0.0).astype(BF16)

    def weighted_values_t(p_t, off, n):
        v_t = jnp.concatenate([vt_ref[0, 0, :, pl.ds(off, n)], jnp.broadcast_to(ones_row, (V_EXTRA_ROWS, n))], axis=0)
        return _dot(v_t, p_t.astype(BF16))

    def query_tile(i, carry):
        base = pl.multiple_of(i * tq, tq)
        acc_t_ref[...] = jnp.zeros(acc_t_ref.shape, F32)

        def full_step(j, c):
            off = pl.multiple_of(j * step, step)
            s_t = _dot_nt(k_ref[0, 0, pl.ds(off, step), :], q_ref[0, 0, pl.ds(base, tq), :])
            acc_t_ref[...] += weighted_values_t(jnp.exp2(s_t), off, step)
            return c

        lax.fori_loop(0, i * (tq // step), full_step, 0)

        for r in range(tq // dg):
            cols = slice(r * dg, (r + 1) * dg)
            q_rows = pl.ds(pl.multiple_of(base + r * dg, dg), dg)
            n = (r + 1) * dg
            s_t = _dot_nt(k_ref[0, 0, pl.ds(base, n), :], q_ref[0, 0, q_rows, :])
            visible = (lax.broadcasted_iota(jnp.int32, (n, dg), 0)
                       <= lax.broadcasted_iota(jnp.int32, (n, dg), 1) + r * dg)
            acc_t = acc_t_ref[:, cols] + weighted_values_t(jnp.exp2(jnp.where(visible, s_t, NEG)), base, n)
            o_ref[0, q_rows, :] = jnp.transpose(acc_t[:V_DIM, :] / acc_t[V_DIM:V_DIM + 1, :]).astype(BF16)
        return carry

    lax.fori_loop(0, q_ref.shape[2] // tq, query_tile, 0)


def _flash_running_max(q_ref, k_ref, vt_ref, o_ref, m_ref, l_ref, acc_ref):
    tq, tk = Q_TILE, FALLBACK_KV_TILE
    row = lax.broadcasted_iota(jnp.int32, (tq, tk), 0)
    col = lax.broadcasted_iota(jnp.int32, (tq, tk), 1)

    def query_tile(i, carry):
        base = pl.multiple_of(i * tq, tq)
        m_ref[...] = jnp.full(m_ref.shape, NEG, F32)
        l_ref[...] = jnp.zeros(l_ref.shape, F32)
        acc_ref[...] = jnp.zeros(acc_ref.shape, F32)

        def step(off, mask):
            s = _dot_nt(q_ref[0, 0, pl.ds(base, tq), :], k_ref[0, 0, pl.ds(off, tk), :])
            if mask is not None:
                s = jnp.where(mask, s, NEG)
            m_old = m_ref[...]
            m_new = jnp.maximum(m_old, jnp.max(s, axis=-1, keepdims=True))
            alpha = jnp.exp2(m_old - m_new)
            p = jnp.exp2(s - m_new)
            l_ref[...] = alpha * l_ref[...] + jnp.sum(p, axis=-1, keepdims=True)
            acc_ref[...] = alpha * acc_ref[...] + _dot_nt(p.astype(BF16), vt_ref[0, 0, :, pl.ds(off, tk)])
            m_ref[...] = m_new

        def full_step(j, c):
            step(pl.multiple_of(j * tk, tk), None)
            return c

        lax.fori_loop(0, i * (tq // tk), full_step, 0)
        for d in range(tq // tk):
            step(pl.multiple_of(base + d * tk, tk), row >= col + d * tk)
        o_ref[0, pl.ds(base, tq), :] = (acc_ref[...] / l_ref[...]).astype(BF16)
        return carry

    lax.fori_loop(0, q_ref.shape[2] // tq, query_tile, 0)


def _flash_kernel(fixed_ref_ok, q_ref, k_ref, vt_ref, o_ref, acc_t_ref, m_ref, l_ref, acc_ref):
    @pl.when(fixed_ref_ok[0] == 1)
    def _():
        _flash_fixed_ref(q_ref, k_ref, vt_ref, o_ref, acc_t_ref)

    @pl.when(fixed_ref_ok[0] == 0)
    def _():
        _flash_running_max(q_ref, k_ref, vt_ref, o_ref, m_ref, l_ref, acc_ref)


def _flash_call(fixed_ref_ok, q, k, v_t):
    b, nh, s, _ = q.shape
    tq = Q_TILE
    grid_spec = pltpu.PrefetchScalarGridSpec(
        num_scalar_prefetch=1,
        grid=(b, nh),
        in_specs=[
            pl.BlockSpec((1, 1, s, QK_PAD), lambda bi, h, _: (bi, h, 0, 0)),
            pl.BlockSpec((1, 1, s, QK_PAD), lambda bi, h, _: (bi, h, 0, 0)),
            pl.BlockSpec((1, 1, V_DIM, s), lambda bi, h, _: (bi, h, 0, 0)),
        ],
        out_specs=pl.BlockSpec((1, s, V_DIM), lambda bi, h, _: (bi, 0, h)),
        scratch_shapes=[pltpu.VMEM((V_DIM + V_EXTRA_ROWS, tq), F32),
                        pltpu.VMEM((tq, 1), F32), pltpu.VMEM((tq, 1), F32), pltpu.VMEM((tq, V_DIM), F32)],
    )
    return pl.pallas_call(
        _flash_kernel,
        grid_spec=grid_spec,
        out_shape=jax.ShapeDtypeStruct((b, s, nh * V_DIM), BF16),
        compiler_params=pltpu.CompilerParams(
            dimension_semantics=("parallel", "parallel"), vmem_limit_bytes=VMEM_LIMIT),
        name="flash_attn",
    )(fixed_ref_ok, q, k, v_t)


def _attn_out_mlp_kernel(x_ref, o_ref, w_o_ref, g_ref, w1_ref, w2_ref, out_ref):
    x1 = x_ref[...] + _dot(o_ref[...], w_o_ref[...])
    out_ref[...] = _mlp_tail(x1, g_ref, w1_ref, w2_ref)


def _attn_out_mlp_call(x2d, o2d, w_o, g_mlp, w1, w2, attn_layer, layer):
    t, _ = x2d.shape
    tm = TOKEN_TILE
    return pl.pallas_call(
        _attn_out_mlp_kernel,
        grid=(t // tm,),
        in_specs=[
            pl.BlockSpec((tm, D_MODEL), lambda i: (i, 0)),
            pl.BlockSpec((tm, N_HEADS * V_DIM), lambda i: (i, 0)),
            _resident_layer(w_o, attn_layer), _resident((1, D_MODEL)),
            _resident_layer(w1, layer), _resident_layer(w2, layer),
        ],
        out_specs=pl.BlockSpec((tm, D_MODEL), lambda i: (i, 0)),
        out_shape=jax.ShapeDtypeStruct((t, D_MODEL), F32),
        compiler_params=pltpu.CompilerParams(dimension_semantics=("parallel",), vmem_limit_bytes=VMEM_LIMIT),
        name="attn_out_mlp",
    )(x2d, o2d, w_o, g_mlp, w1, w2)


def _conv_mlp_kernel(x_ref, g_mix_ref, w_in_ref, cw_ref, w_out_ref, g_ref, w1_ref, w2_ref, out_ref, u_buf):
    tm = x_ref.shape[1]

    @pl.when(pl.program_id(1) == 0)
    def _():
        u_buf[:HALO, :] = jnp.zeros((HALO, D_MODEL), F32)

    x = x_ref[0]
    h = (_rms(x) * g_mix_ref[...]).astype(BF16)
    gate_c = _dot(h, w_in_ref[:, D_MODEL:2 * D_MODEL])
    u = gate_c * _dot(h, w_in_ref[:, 2 * D_MODEL:])
    u_buf[HALO:, :] = u
    conv = (u_buf[HALO - 2:HALO - 2 + tm, :] * cw_ref[0:1, :]
            + u_buf[HALO - 1:HALO - 1 + tm, :] * cw_ref[1:2, :]
            + u * cw_ref[2:3, :])
    u_buf[:HALO, :] = u_buf[tm:tm + HALO, :]
    gate_b = _dot(h, w_in_ref[:, :D_MODEL])
    x1 = x + _dot((gate_b * conv).astype(BF16), w_out_ref[...])
    out_ref[0] = _mlp_tail(x1, g_ref, w1_ref, w2_ref)


def _conv_mlp_call(x, g_mix, w_in, conv_w, w_out, g_mlp, w1, w2, conv_layer, layer):
    b, s, _ = x.shape
    tm = TOKEN_TILE
    return pl.pallas_call(
        _conv_mlp_kernel,
        grid=(b, s // tm),
        in_specs=[
            pl.BlockSpec((1, tm, D_MODEL), lambda bi, i: (bi, i, 0)),
            _resident((1, D_MODEL)), _resident_layer(w_in, conv_layer), _resident_layer(conv_w, conv_layer),
            _resident_layer(w_out, conv_layer),
            _resident((1, D_MODEL)), _resident_layer(w1, layer), _resident_layer(w2, layer),
        ],
        out_specs=pl.BlockSpec((1, tm, D_MODEL), lambda bi, i: (bi, i, 0)),
        out_shape=jax.ShapeDtypeStruct((b, s, D_MODEL), F32),
        scratch_shapes=[pltpu.VMEM((tm + HALO, D_MODEL), F32)],
        compiler_params=pltpu.CompilerParams(
            dimension_semantics=("parallel", "arbitrary"), vmem_limit_bytes=VMEM_LIMIT),
        name="conv_mlp",
    )(x, g_mix, w_in, conv_w, w_out, g_mlp, w1, w2)


def _swap_halves(w):
    half = w.shape[-1] // 2
    return jnp.concatenate([w[..., half:], w[..., :half]], axis=-1)


def _attn_params(g_mix, w_down, g_q_a, g_kv_a, w_uq, w_ukv, g_qnorm, g_knorm):
    pad = lambda w: jnp.pad(w, [(0, 0)] * (w.ndim - 1) + [(0, LANES - w.shape[-1])])
    kpe = w_down[:, Q_LORA + KV_LORA:]
    w_down_ext = jnp.concatenate([w_down[:, :Q_LORA + KV_LORA], pad(kpe), pad(_swap_halves(kpe))], axis=1)
    w_down_ext = g_mix[:, None] * w_down_ext
    uq = w_uq.reshape(Q_LORA, N_HEADS, QK_HEAD_DIM)
    uq_r = uq[:, :, NOPE:]
    w_q = jnp.concatenate([uq[:, :, :NOPE].reshape(Q_LORA, -1), pad(uq_r).reshape(Q_LORA, -1),
                           pad(_swap_halves(uq_r)).reshape(Q_LORA, -1)], axis=1)
    ukv = w_ukv.reshape(KV_LORA, N_HEADS, NOPE + V_DIM)
    w_kn = ukv[:, :, :NOPE].reshape(KV_LORA, -1)
    w_vt = ukv[:, :, NOPE:].reshape(KV_LORA, -1).T
    row = lambda g: pad(g)[None, :]
    bound = (BOUND_MARGIN * Q_SCALE * QK_HEAD_DIM) * jnp.max(jnp.abs(g_qnorm)) * jnp.max(jnp.abs(g_knorm))
    fixed_ref_ok = (2.0 * bound < MAX_FIXED_REF_SPAN).astype(jnp.int32).reshape(1)
    q_pad = jnp.where(jnp.arange(LANES) == ROPE, -bound, 0.0)[None, :]
    return {
        'w_down': w_down_ext.astype(BF16), 'g_qa': g_q_a[None, :], 'g_kva': g_kv_a[None, :],
        'w_q': w_q.astype(BF16), 'w_kn': w_kn.astype(BF16), 'w_vt': w_vt.astype(BF16),
        'gq_n': g_qnorm[None, :NOPE], 'gq_r': row(g_qnorm[NOPE:]), 'gq_rs': row(_swap_halves(g_qnorm[NOPE:])),
        'gk_n': g_knorm[None, :NOPE], 'gk_r': row(g_knorm[NOPE:]), 'gk_rs': row(_swap_halves(g_knorm[NOPE:])),
        'q_pad': q_pad, 'fixed_ref_ok': fixed_ref_ok,
    }


def _rope_tables(positions):
    inv_freq = ROPE_THETA ** (-jnp.arange(0, ROPE, 2, dtype=F32) / ROPE)
    ang = positions.astype(F32)[:, None, :] * inv_freq[None, :, None]
    cos, sin = jnp.cos(ang), jnp.sin(ang)
    zeros = jnp.zeros((positions.shape[0], LANES - ROPE, positions.shape[1]), F32)
    return jnp.concatenate([cos, cos, zeros], axis=1), jnp.concatenate([-sin, sin, zeros], axis=1)


def kernel(x, positions, g_mix, g_mlp, attn_w_down, attn_g_q_a, attn_g_kv_a, attn_w_uq, attn_w_ukv, attn_g_qnorm,
           attn_g_knorm, attn_w_o, conv_w_in, conv_w, conv_w_out, mlp_w1, mlp_w2):
    b, s, d = x.shape
    depth = g_mix.shape[0]
    cos2, sin2 = _rope_tables(positions)
    w1, w2 = mlp_w1.astype(BF16), mlp_w2.astype(BF16)
    w_o, w_in, w_out = attn_w_o.astype(BF16), conv_w_in.astype(BF16), conv_w_out.astype(BF16)
    for i in range(depth):
        j = i // 2
        if i % 2 == 0:
            w = _attn_params(g_mix[i], attn_w_down[j], attn_g_q_a[j], attn_g_kv_a[j], attn_w_uq[j], attn_w_ukv[j],
                             attn_g_qnorm[j], attn_g_knorm[j])
            q, k, v_t = _qkv_call(x, cos2, sin2, w)
            o = _flash_call(w['fixed_ref_ok'], q, k, v_t)
            x = _attn_out_mlp_call(x.reshape(b * s, d), o.reshape(b * s, d), w_o, g_mlp[i][None, :], w1, w2,
                                   j, i).reshape(b, s, d)
        else:
            x = _conv_mlp_call(x, g_mix[i][None, :], w_in, conv_w, w_out, g_mlp[i][None, :], w1, w2, j, i)
    return x
```

```python
import functools
import math

import jax
import jax.numpy as jnp
from jax import lax
from jax.experimental import pallas as pl
from jax.experimental.pallas import tpu as pltpu

D_MODEL = 1024
N_HEADS = 8
NOPE = 128
ROPE = 64
QK_HEAD_DIM = NOPE + ROPE
V_DIM = 128
Q_LORA = 256
KV_LORA = 128
D_FF = 4096
CONV_WIDTH = 3
EPS = 1e-6
ROPE_THETA = 10000.0

LANES = 128
BF16_SUBLANES = 16
QK_PAD = 2 * LANES
V_EXTRA_ROWS = 16
HALO = 8
NEG = -1e30

Q_SCALE = QK_HEAD_DIM ** -0.5 * math.log2(math.e)
MAX_ABS_LOGIT = 50.0
BOUND_MARGIN = 1.02

TOKEN_TILE = 512
QKV_TILE = 1024
QKV_SUB_TILE = 512
FF_CHUNK = 1024
Q_TILE = 2048
KV_STEP = 2048
DIAG_TILE = 512
FALLBACK_KV_TILE = 512
VMEM_LIMIT = 56 * 1024 * 1024

F32 = jnp.float32
BF16 = jnp.bfloat16


def _rms(x):
    return x * lax.rsqrt(jnp.mean(x * x, axis=-1, keepdims=True) + EPS)


def _dot(a, b):
    return jnp.dot(a, b, preferred_element_type=F32)


def _resident(shape):
    return pl.BlockSpec(shape, lambda *_: (0,) * len(shape), pipeline_mode=pl.Buffered(1))


def _resident_layer(stacked, layer):
    tail = stacked.shape[1:]
    return pl.BlockSpec((None,) + tail, lambda *_: (layer,) + (0,) * len(tail), pipeline_mode=pl.Buffered(1))


def _cast_job_specs(jobs, grid):
    steps = math.prod(grid)

    def step_index(*ids):
        index = ids[0]
        for extent, i in zip(grid[1:], ids[1:len(grid)]):
            index = index * extent + i
        return index

    in_specs, operands, out_specs, out_shapes = [], [], [], []
    for stacked, layer, _ in jobs:
        _, rows, cols = stacked.shape
        assert rows % (steps * BF16_SUBLANES) == 0, (rows, steps)
        in_specs.append(pl.BlockSpec((None, rows // steps, cols),
                                     lambda *ids, layer=layer: (layer, step_index(*ids), 0)))
        operands.append(stacked)
        out_specs.append(pl.BlockSpec((rows // steps, cols), lambda *ids: (step_index(*ids), 0)))
        out_shapes.append(jax.ShapeDtypeStruct((rows, cols), BF16))
    for stacked, _, gain in jobs:
        if gain is not None:
            in_specs.append(pl.BlockSpec((stacked.shape[1] // steps, 1), lambda *ids: (step_index(*ids), 0)))
            operands.append(gain[:, None])
    return in_specs, operands, out_specs, out_shapes, tuple(gain is not None for _, _, gain in jobs)


def _run_cast_jobs(in_refs, dst_refs, scaled):
    gain_refs = iter(in_refs[len(scaled):])
    for src, dst, has_gain in zip(in_refs, dst_refs, scaled):
        w = src[...]
        if has_gain:
            w = w * next(gain_refs)[...]
        dst[...] = w.astype(BF16)


def _rms_scale(x):
    return lax.rsqrt(jnp.mean(x * x, axis=-1, keepdims=True) + EPS)


def _mlp_tail(x1, w1g_ref, w2_ref):
    xb = x1.astype(BF16)
    acc = None
    for c in range(D_FF // FF_CHUNK):
        cols = slice(c * FF_CHUNK, (c + 1) * FF_CHUNK)
        u = jnp.maximum(_dot(xb, w1g_ref[:, cols]), 0.0)
        part = _dot((u * u).astype(BF16), w2_ref[cols, :])
        acc = part if acc is None else acc + part
    r = _rms_scale(x1)
    return x1 + (r * r) * acc


def _dot_nt(a, b):
    return lax.dot_general(a, b, (((1,), (1,)), ((), ())), preferred_element_type=F32)


def _qkv_kernel(x_ref, cos_ref, sin_ref, w_down_ref, g_qa_ref, g_kva_ref, w_q_ref, w_kn_ref, w_vt_ref,
                gq_r_ref, gq_rs_ref, gqk_n_ref, gk_r_ref, gk_rs_ref, q_ref, k_ref, vt_ref, *, q_scale):
    q_off = N_HEADS * NOPE
    d_eps = QK_HEAD_DIM * EPS
    sqrt_d = math.sqrt(QK_HEAD_DIM)
    gq_r, gq_rs = (g[...] * (sqrt_d * q_scale) for g in (gq_r_ref, gq_rs_ref))
    gk_r, gk_rs = (g[...] * sqrt_d for g in (gk_r_ref, gk_rs_ref))
    gqk_n = gqk_n_ref[...] * (QK_HEAD_DIM * q_scale)

    for t in range(x_ref.shape[1] // QKV_SUB_TILE):
        rows = slice(t * QKV_SUB_TILE, (t + 1) * QKV_SUB_TILE)
        x = x_ref[0, rows, :]
        a = _dot(x.astype(BF16), w_down_ref[...]) * lax.rsqrt(jnp.mean(x * x, axis=-1, keepdims=True) + EPS)
        c_q = (_rms(a[:, :Q_LORA]) * g_qa_ref[...]).astype(BF16)
        c_kv = (_rms(a[:, Q_LORA:Q_LORA + KV_LORA]) * g_kva_ref[...]).astype(BF16)
        kpe = a[:, Q_LORA + KV_LORA:Q_LORA + KV_LORA + LANES]
        kpe_sw = a[:, Q_LORA + KV_LORA + LANES:]
        qa = _dot(c_q, w_q_ref[...])
        kna = _dot(c_kv, w_kn_ref[...])
        vta = _dot_nt(w_vt_ref[...], c_kv)

        cos2 = jnp.transpose(cos_ref[0, :, rows])
        sin2 = jnp.transpose(sin_ref[0, :, rows])
        k_rot = kpe * (gk_r * cos2) + kpe_sw * (gk_rs * sin2)
        kpe_sq = kpe * kpe
        q_cos = gq_r * cos2
        q_sin = gq_rs * sin2

        for hd in range(N_HEADS):
            qn = qa[:, hd * NOPE:(hd + 1) * NOPE]
            qr = qa[:, q_off + hd * LANES:q_off + (hd + 1) * LANES]
            qr_sw = qa[:, q_off + (N_HEADS + hd) * LANES:q_off + (N_HEADS + hd + 1) * LANES]
            r = lax.rsqrt(jnp.sum(qn * qn + qr * qr, axis=-1, keepdims=True) + d_eps)
            q_ref[0, hd, rows, :NOPE] = (qn * r).astype(BF16)
            q_ref[0, hd, rows, NOPE:] = ((qr * q_cos + qr_sw * q_sin) * r).astype(BF16)

            kn = kna[:, hd * NOPE:(hd + 1) * NOPE]
            r = lax.rsqrt(jnp.sum(kn * kn + kpe_sq, axis=-1, keepdims=True) + d_eps)
            k_ref[0, hd, rows, :NOPE] = (kn * r * gqk_n).astype(BF16)
            k_ref[0, hd, rows, NOPE:] = (k_rot * r).astype(BF16)
            vt_ref[0, hd, :, rows] = vta[hd * V_DIM:(hd + 1) * V_DIM, :].astype(BF16)


def _qkv_call(x, cos2, sin2, w):
    b, s, _ = x.shape
    tm = QKV_TILE
    row = lambda n: _resident((1, n))
    out_shape = (jax.ShapeDtypeStruct((b, N_HEADS, s, QK_PAD), BF16),
                 jax.ShapeDtypeStruct((b, N_HEADS, s, QK_PAD), BF16),
                 jax.ShapeDtypeStruct((b, N_HEADS, V_DIM, s), BF16))
    return pl.pallas_call(
        functools.partial(_qkv_kernel, q_scale=Q_SCALE),
        grid=(b, s // tm),
        in_specs=[
            pl.BlockSpec((1, tm, D_MODEL), lambda bi, i: (bi, i, 0)),
            pl.BlockSpec((1, LANES, tm), lambda bi, i: (bi, 0, i)),
            pl.BlockSpec((1, LANES, tm), lambda bi, i: (bi, 0, i)),
            _resident(w['w_down'].shape), row(Q_LORA), row(KV_LORA),
            _resident(w['w_q'].shape), _resident(w['w_kn'].shape), _resident(w['w_vt'].shape),
            row(LANES), row(LANES), row(NOPE), row(LANES), row(LANES),
        ],
        out_specs=(
            pl.BlockSpec((1, N_HEADS, tm, QK_PAD), lambda bi, i: (bi, 0, i, 0)),
            pl.BlockSpec((1, N_HEADS, tm, QK_PAD), lambda bi, i: (bi, 0, i, 0)),
            pl.BlockSpec((1, N_HEADS, V_DIM, tm), lambda bi, i: (bi, 0, 0, i)),
        ),
        out_shape=out_shape,
        compiler_params=pltpu.CompilerParams(
            dimension_semantics=("parallel", "parallel"), vmem_limit_bytes=VMEM_LIMIT),
        name="qkv_proj",
    )(x, cos2, sin2, w['w_down'], w['g_qa'], w['g_kva'], w['w_q'], w['w_kn'], w['w_vt'],
      w['gq_r'], w['gq_rs'], w['gqk_n'], w['gk_r'], w['gk_rs'])


def _flash_fixed_ref(q_ref, k_ref, vt_ref, o_ref, acc_t_ref):
    tq, step, dg = Q_TILE, KV_STEP, DIAG_TILE
    ones_row = jnp.where(lax.broadcasted_iota(jnp.int32, (V_EXTRA_ROWS, 1), 0) == 0, 1.0, 0.0).astype(BF16)

    def weighted_values_t(p_t, off, n):
        v_t = jnp.concatenate([vt_ref[0, 0, :, pl.ds(off, n)], jnp.broadcast_to(ones_row, (V_EXTRA_ROWS, n))], axis=0)
        return _dot(v_t, p_t.astype(BF16))

    def query_tile(i, carry):
        base = pl.multiple_of(i * tq, tq)
        acc_t_ref[...] = jnp.zeros(acc_t_ref.shape, F32)

        def full_step(j, c):
            off = pl.multiple_of(j * step, step)
            s_t = _dot_nt(k_ref[0, 0, pl.ds(off, step), :], q_ref[0, 0, pl.ds(base, tq), :])
            acc_t_ref[...] += weighted_values_t(jnp.exp2(s_t), off, step)
            return c

        lax.fori_loop(0, i * (tq // step), full_step, 0)

        for r in range(tq // dg):
            cols = slice(r * dg, (r + 1) * dg)
            q_rows = pl.ds(pl.multiple_of(base + r * dg, dg), dg)
            n = (r + 1) * dg
            s_t = _dot_nt(k_ref[0, 0, pl.ds(base, n), :], q_ref[0, 0, q_rows, :])
            visible = (lax.broadcasted_iota(jnp.int32, (n, dg), 0)
                       <= lax.broadcasted_iota(jnp.int32, (n, dg), 1) + r * dg)
            acc_t = acc_t_ref[:, cols] + weighted_values_t(jnp.exp2(jnp.where(visible, s_t, NEG)), base, n)
            o_ref[0, q_rows, :] = jnp.transpose(acc_t[:V_DIM, :] / acc_t[V_DIM:V_DIM + 1, :]).astype(BF16)
        return carry

    lax.fori_loop(0, q_ref.shape[2] // tq, query_tile, 0)


def _flash_running_max(q_ref, k_ref, vt_ref, o_ref, m_ref, l_ref, acc_ref):
    tq, tk = Q_TILE, FALLBACK_KV_TILE
    row = lax.broadcasted_iota(jnp.int32, (tq, tk), 0)
    col = lax.broadcasted_iota(jnp.int32, (tq, tk), 1)

    def query_tile(i, carry):
        base = pl.multiple_of(i * tq, tq)
        m_ref[...] = jnp.full(m_ref.shape, NEG, F32)
        l_ref[...] = jnp.zeros(l_ref.shape, F32)
        acc_ref[...] = jnp.zeros(acc_ref.shape, F32)

        def step(off, mask):
            s = _dot_nt(q_ref[0, 0, pl.ds(base, tq), :], k_ref[0, 0, pl.ds(off, tk), :])
            if mask is not None:
                s = jnp.where(mask, s, NEG)
            m_old = m_ref[...]
            m_new = jnp.maximum(m_old, jnp.max(s, axis=-1, keepdims=True))
            alpha = jnp.exp2(m_old - m_new)
            p = jnp.exp2(s - m_new)
            l_ref[...] = alpha * l_ref[...] + jnp.sum(p, axis=-1, keepdims=True)
            acc_ref[...] = alpha * acc_ref[...] + _dot_nt(p.astype(BF16), vt_ref[0, 0, :, pl.ds(off, tk)])
            m_ref[...] = m_new

        def full_step(j, c):
            step(pl.multiple_of(j * tk, tk), None)
            return c

        lax.fori_loop(0, i * (tq // tk), full_step, 0)
        for d in range(tq // tk):
            step(pl.multiple_of(base + d * tk, tk), row >= col + d * tk)
        o_ref[0, pl.ds(base, tq), :] = (acc_ref[...] / l_ref[...]).astype(BF16)
        return carry

    lax.fori_loop(0, q_ref.shape[2] // tq, query_tile, 0)


def _flash_kernel(fixed_ref_ok, q_ref, k_ref, vt_ref, *refs, scaled):
    n_in, n_out = len(scaled) + sum(scaled), len(scaled)
    cast_in, o_ref, cast_dst = refs[:n_in], refs[n_in], refs[n_in + 1:n_in + 1 + n_out]
    acc_t_ref, m_ref, l_ref, acc_ref = refs[n_in + 1 + n_out:]
    _run_cast_jobs(cast_in, cast_dst, scaled)

    @pl.when(fixed_ref_ok[0] == 1)
    def _():
        _flash_fixed_ref(q_ref, k_ref, vt_ref, o_ref, acc_t_ref)

    @pl.when(fixed_ref_ok[0] == 0)
    def _():
        _flash_running_max(q_ref, k_ref, vt_ref, o_ref, m_ref, l_ref, acc_ref)


def _flash_call(fixed_ref_ok, q, k, v_t, cast_jobs):
    b, nh, s, _ = q.shape
    tq = Q_TILE
    grid = (b, nh)
    cast_in, cast_operands, cast_out, cast_shapes, scaled = _cast_job_specs(cast_jobs, grid)
    grid_spec = pltpu.PrefetchScalarGridSpec(
        num_scalar_prefetch=1,
        grid=grid,
        in_specs=[
            pl.BlockSpec((1, 1, s, QK_PAD), lambda bi, h, _: (bi, h, 0, 0)),
            pl.BlockSpec((1, 1, s, QK_PAD), lambda bi, h, _: (bi, h, 0, 0)),
            pl.BlockSpec((1, 1, V_DIM, s), lambda bi, h, _: (bi, h, 0, 0)),
        ] + cast_in,
        out_specs=[pl.BlockSpec((1, s, V_DIM), lambda bi, h, _: (bi, 0, h))] + cast_out,
        scratch_shapes=[pltpu.VMEM((V_DIM + V_EXTRA_ROWS, tq), F32),
                        pltpu.VMEM((tq, 1), F32), pltpu.VMEM((tq, 1), F32), pltpu.VMEM((tq, V_DIM), F32)],
    )
    o, *casted = pl.pallas_call(
        functools.partial(_flash_kernel, scaled=scaled),
        grid_spec=grid_spec,
        out_shape=[jax.ShapeDtypeStruct((b, s, nh * V_DIM), BF16)] + cast_shapes,
        compiler_params=pltpu.CompilerParams(
            dimension_semantics=("parallel", "parallel"), vmem_limit_bytes=VMEM_LIMIT),
        name="flash_attn",
    )(fixed_ref_ok, q, k, v_t, *cast_operands)
    return o, casted


def _attn_out_mlp_kernel(x_ref, o_ref, w_o_ref, w1g_ref, w2_ref, *refs, scaled):
    n_in = len(scaled) + sum(scaled)
    cast_in, out_ref, cast_dst = refs[:n_in], refs[n_in], refs[n_in + 1:]
    _run_cast_jobs(cast_in, cast_dst, scaled)
    x1 = x_ref[...] + _dot(o_ref[...], w_o_ref[...])
    out_ref[...] = _mlp_tail(x1, w1g_ref, w2_ref)


def _attn_out_mlp_call(x2d, o2d, w_o, w1g, w2, cast_jobs):
    t, _ = x2d.shape
    tm = TOKEN_TILE
    grid = (t // tm,)
    cast_in, cast_operands, cast_out, cast_shapes, scaled = _cast_job_specs(cast_jobs, grid)
    out, *casted = pl.pallas_call(
        functools.partial(_attn_out_mlp_kernel, scaled=scaled),
        grid=grid,
        in_specs=[
            pl.BlockSpec((tm, D_MODEL), lambda i: (i, 0)),
            pl.BlockSpec((tm, N_HEADS * V_DIM), lambda i: (i, 0)),
            _resident(w_o.shape), _resident(w1g.shape), _resident(w2.shape),
        ] + cast_in,
        out_specs=[pl.BlockSpec((tm, D_MODEL), lambda i: (i, 0))] + cast_out,
        out_shape=[jax.ShapeDtypeStruct((t, D_MODEL), F32)] + cast_shapes,
        compiler_params=pltpu.CompilerParams(dimension_semantics=("parallel",), vmem_limit_bytes=VMEM_LIMIT),
        name="attn_out_mlp",
    )(x2d, o2d, w_o, w1g, w2, *cast_operands)
    return out, casted


def _conv_mlp_kernel(x_ref, w_in_g_ref, cw_ref, w_out_ref, w1g_ref, w2_ref, out_ref, u_buf):
    tm = x_ref.shape[1]

    @pl.when(pl.program_id(1) == 0)
    def _():
        u_buf[:HALO, :] = jnp.zeros((HALO, D_MODEL), F32)

    x = x_ref[0]
    xb = x.astype(BF16)
    r = _rms_scale(x)
    u = (_dot(xb, w_in_g_ref[:, D_MODEL:2 * D_MODEL]) * _dot(xb, w_in_g_ref[:, 2 * D_MODEL:])) * (r * r)
    u_buf[HALO:, :] = u
    conv = (u_buf[HALO - 2:HALO - 2 + tm, :] * cw_ref[0:1, :]
            + u_buf[HALO - 1:HALO - 1 + tm, :] * cw_ref[1:2, :]
            + u * cw_ref[2:3, :])
    u_buf[:HALO, :] = u_buf[tm:tm + HALO, :]
    gate_b = _dot(xb, w_in_g_ref[:, :D_MODEL]) * r
    x1 = x + _dot((gate_b * conv).astype(BF16), w_out_ref[...])
    out_ref[0] = _mlp_tail(x1, w1g_ref, w2_ref)


def _conv_mlp_call(x, w_in_g, conv_w, conv_layer, w_out, w1g, w2):
    b, s, _ = x.shape
    tm = TOKEN_TILE
    return pl.pallas_call(
        _conv_mlp_kernel,
        grid=(b, s // tm),
        in_specs=[
            pl.BlockSpec((1, tm, D_MODEL), lambda bi, i: (bi, i, 0)),
            _resident(w_in_g.shape), _resident_layer(conv_w, conv_layer), _resident(w_out.shape),
            _resident(w1g.shape), _resident(w2.shape),
        ],
        out_specs=pl.BlockSpec((1, tm, D_MODEL), lambda bi, i: (bi, i, 0)),
        out_shape=jax.ShapeDtypeStruct((b, s, D_MODEL), F32),
        scratch_shapes=[pltpu.VMEM((tm + HALO, D_MODEL), F32)],
        compiler_params=pltpu.CompilerParams(
            dimension_semantics=("parallel", "arbitrary"), vmem_limit_bytes=VMEM_LIMIT),
        name="conv_mlp",
    )(x, w_in_g, conv_w, w_out, w1g, w2)


def _swap_halves(w):
    half = w.shape[-1] // 2
    return jnp.concatenate([w[..., half:], w[..., :half]], axis=-1)


def _attn_params(g_mix, w_down, g_q_a, g_kv_a, w_uq, w_ukv, g_qnorm, g_knorm):
    pad = lambda w: jnp.pad(w, [(0, 0)] * (w.ndim - 1) + [(0, LANES - w.shape[-1])])
    kpe = w_down[:, Q_LORA + KV_LORA:]
    w_down_ext = jnp.concatenate([w_down[:, :Q_LORA + KV_LORA], pad(kpe), pad(_swap_halves(kpe))], axis=1)
    w_down_ext = g_mix[:, None] * w_down_ext
    uq = w_uq.reshape(Q_LORA, N_HEADS, QK_HEAD_DIM)
    uq_r = uq[:, :, NOPE:]
    w_q = jnp.concatenate([uq[:, :, :NOPE].reshape(Q_LORA, -1), pad(uq_r).reshape(Q_LORA, -1),
                           pad(_swap_halves(uq_r)).reshape(Q_LORA, -1)], axis=1)
    ukv = w_ukv.reshape(KV_LORA, N_HEADS, NOPE + V_DIM)
    w_kn = ukv[:, :, :NOPE].reshape(KV_LORA, -1)
    w_vt = ukv[:, :, NOPE:].reshape(KV_LORA, -1).T
    row = lambda g: pad(g)[None, :]
    bound = (BOUND_MARGIN * Q_SCALE * QK_HEAD_DIM) * jnp.max(jnp.abs(g_qnorm)) * jnp.max(jnp.abs(g_knorm))
    fixed_ref_ok = (bound < MAX_ABS_LOGIT).astype(jnp.int32).reshape(1)
    return {
        'w_down': w_down_ext.astype(BF16), 'g_qa': g_q_a[None, :], 'g_kva': g_kv_a[None, :],
        'w_q': w_q.astype(BF16), 'w_kn': w_kn.astype(BF16), 'w_vt': w_vt.astype(BF16),
        'gq_r': row(g_qnorm[NOPE:]), 'gq_rs': row(_swap_halves(g_qnorm[NOPE:])),
        'gqk_n': (g_qnorm[:NOPE] * g_knorm[:NOPE])[None, :],
        'gk_r': row(g_knorm[NOPE:]), 'gk_rs': row(_swap_halves(g_knorm[NOPE:])),
        'fixed_ref_ok': fixed_ref_ok,
    }


def _rope_tables(positions):
    inv_freq = ROPE_THETA ** (-jnp.arange(0, ROPE, 2, dtype=F32) / ROPE)
    ang = positions.astype(F32)[:, None, :] * inv_freq[None, :, None]
    cos, sin = jnp.cos(ang), jnp.sin(ang)
    zeros = jnp.zeros((positions.shape[0], LANES - ROPE, positions.shape[1]), F32)
    return jnp.concatenate([cos, cos, zeros], axis=1), jnp.concatenate([-sin, sin, zeros], axis=1)


def kernel(x, positions, g_mix, g_mlp, attn_w_down, attn_g_q_a, attn_g_kv_a, attn_w_uq, attn_w_ukv, attn_g_qnorm,
           attn_g_knorm, attn_w_o, conv_w_in, conv_w, conv_w_out, mlp_w1, mlp_w2):
    b, s, d = x.shape
    depth = g_mix.shape[0]
    cos2, sin2 = _rope_tables(positions)
    conv_weights = None
    for i in range(depth):
        j = i // 2
        if i % 2 == 0:
            w = _attn_params(g_mix[i], attn_w_down[j], attn_g_q_a[j], attn_g_kv_a[j], attn_w_uq[j], attn_w_ukv[j],
                             attn_g_qnorm[j], attn_g_knorm[j])
            q, k, v_t = _qkv_call(x, cos2, sin2, w)
            o, (w_o, w1g, w2) = _flash_call(w['fixed_ref_ok'], q, k, v_t,
                                            [(attn_w_o, j, None), (mlp_w1, i, g_mlp[i]), (mlp_w2, i, None)])
            next_jobs = ([(conv_w_in, j, g_mix[i + 1]), (conv_w_out, j, None),
                          (mlp_w1, i + 1, g_mlp[i + 1]), (mlp_w2, i + 1, None)] if i + 1 < depth else [])
            x, conv_weights = _attn_out_mlp_call(x.reshape(b * s, d), o.reshape(b * s, d), w_o, w1g, w2, next_jobs)
            x = x.reshape(b, s, d)
        else:
            w_in_g, w_out, w1g, w2 = conv_weights
            x = _conv_mlp_call(x, w_in_g, conv_w, j, w_out, w1g, w2)
    return x
```
